```python
import math
import jax, jax.numpy as jnp
from jax import lax
import numpy as np

D_MODEL = 1024
BATCH = 8
SEQ = 8192
DEPTH = 2

MIX_WIDTH = D_MODEL
ATTN_WIDTH = MIX_WIDTH // 2
SSM_WIDTH = MIX_WIDTH - ATTN_WIDTH
DIFF_HEAD_DIM = 64
DIFF_V_DIM = 2 * DIFF_HEAD_DIM
DIFF_N_HEADS = ATTN_WIDTH // DIFF_V_DIM
QK_WIDTH = DIFF_N_HEADS * 2 * DIFF_HEAD_DIM
V_WIDTH = DIFF_N_HEADS * DIFF_V_DIM
Q_BLOCK = 128
ROPE_THETA = 10000.0
SSM_GROUP = 16
SSM_N_GROUPS = SSM_WIDTH // SSM_GROUP
SSM_STATE = 64
DT_MIN = 0.001
DT_MAX = 0.1
IN_WIDTH = 2 * QK_WIDTH + V_WIDTH + SSM_WIDTH
D_FF = 4 * D_MODEL
ALPHA = (2.0 * DEPTH) ** 0.25
BETA = (8.0 * DEPTH) ** -0.25
LN_EPS = 1e-5
RMS_EPS = 1e-5

kernel_name = "hybrid_diffattn_s5_deepnorm"


def layer_norm(x, g, b):
    xf = x.astype(jnp.float32)
    mu = jnp.mean(xf, axis=-1, keepdims=True)
    var = jnp.mean(jnp.square(xf - mu), axis=-1, keepdims=True)
    y = (xf - mu) * lax.rsqrt(var + LN_EPS) * g.astype(jnp.float32) + b.astype(jnp.float32)
    return y.astype(x.dtype)


def rms_norm(x, g):
    xf = x.astype(jnp.float32)
    y = xf * lax.rsqrt(jnp.mean(jnp.square(xf), axis=-1, keepdims=True) + RMS_EPS) * g.astype(jnp.float32)
    return y.astype(x.dtype)


def rope_tables(seq_len):
    pos = jnp.arange(seq_len, dtype=jnp.float32)
    inv_freq = ROPE_THETA ** (-jnp.arange(0, DIFF_HEAD_DIM, 2, dtype=jnp.float32) / DIFF_HEAD_DIM)
    ang = pos[:, None] * inv_freq[None, :]
    return jnp.cos(ang), jnp.sin(ang)


def apply_rope(x, cos, sin):
    xf = x.astype(jnp.float32)
    x1, x2 = jnp.split(xf, 2, axis=-1)
    out = jnp.concatenate([x1 * cos - x2 * sin, x2 * cos + x1 * sin], axis=-1)
    return out.astype(x.dtype)


def diff_attention(q, k, v, lam, lam_init, subln_g, cos, sin):
    bsz, seq_len, _ = q.shape
    q = q.reshape(bsz, seq_len, DIFF_N_HEADS, 2, DIFF_HEAD_DIM).transpose(0, 2, 3, 1, 4)
    k = k.reshape(bsz, seq_len, DIFF_N_HEADS, 2, DIFF_HEAD_DIM).transpose(0, 2, 3, 1, 4)
    v = v.reshape(bsz, seq_len, DIFF_N_HEADS, DIFF_V_DIM).transpose(0, 2, 1, 3)
    q = apply_rope(q, cos, sin)
    k = apply_rope(k, cos, sin)
    scale = 1.0 / math.sqrt(DIFF_HEAD_DIM)
    n_blocks = -(-seq_len // Q_BLOCK)
    outs = []
    for i in range(n_blocks):
        q0 = i * Q_BLOCK
        kend = min(q0 + Q_BLOCK, seq_len)
        qb = q[:, :, :, q0:kend]
        kb = k[:, :, :, :kend]
        vb = v[:, :, :kend]
        s = jnp.einsum('bhmqd,bhmkd->bhmqk', qb, kb).astype(jnp.float32) * scale
        qpos = q0 + jnp.arange(kend - q0)
        kpos = jnp.arange(kend)
        mask = kpos[None, :] <= qpos[:, None]
        s = jnp.where(mask, s, -jnp.inf)
        p = jax.nn.softmax(s, axis=-1)
        a = p[:, :, 0] - lam * p[:, :, 1]
        outs.append(jnp.einsum('bhqk,bhkv->bhqv', a.astype(vb.dtype), vb))
    o = jnp.concatenate(outs, axis=2)
    o = rms_norm(o, subln_g) * (1.0 - lam_init)
    return o.transpose(0, 2, 1, 3).reshape(bsz, seq_len, ATTN_WIDTH)


def _scan_op(e_i, e_j):
    a_i, b_i = e_i
    a_j, b_j = e_j
    return a_j * a_i, a_j * b_i + b_j


def s5_ssm(u, lam_re, lam_im, log_dt, b_re, b_im, c_re, c_im, d_skip, glu_w, glu_b):
    bsz, seq_len, _ = u.shape
    f32 = jnp.float32
    uf = u.astype(f32).reshape(bsz, seq_len, SSM_N_GROUPS, SSM_GROUP)
    lam_c = lax.complex(lam_re.astype(f32), lam_im.astype(f32))
    dt = jnp.exp(log_dt.astype(f32))[:, None]
    a_bar = jnp.exp(lam_c * dt)
    b_c = lax.complex(b_re.astype(f32), b_im.astype(f32))
    b_bar = ((a_bar - 1.0) / lam_c)[..., None] * b_c
    bu = jnp.einsum('blgc,gpc->blgp', uf.astype(jnp.complex64), b_bar)
    a_seq = jnp.broadcast_to(a_bar[None, None], (1, seq_len) + a_bar.shape)
    _, states = lax.associative_scan(_scan_op, (a_seq, bu), axis=1)
    c_c = lax.complex(c_re.astype(f32), c_im.astype(f32))
    y = jnp.real(jnp.einsum('blgp,gcp->blgc', states, c_c)) + d_skip.astype(f32) * uf
    y = jax.nn.gelu(y.reshape(bsz, seq_len, SSM_WIDTH))
    y = y * jax.nn.sigmoid(y @ glu_w.astype(f32) + glu_b.astype(f32))
    return y.astype(u.dtype)


def setup_inputs(seed: int = 0) -> dict:
    key = jax.random.key(seed)
    ks = jax.random.split(key, 24)
    f32 = jnp.float32
    nrm = lambda k, shape, s: jax.random.normal(k, shape, f32) * s
    G, P, C = SSM_N_GROUPS, SSM_STATE, SSM_GROUP
    n_idx = jnp.arange(P, dtype=f32)
    return {
        "x": nrm(ks[0], (BATCH, SEQ, D_MODEL), 1.0),
        "w_in": nrm(ks[1], (DEPTH, D_MODEL, IN_WIDTH), D_MODEL ** -0.5),
        "w_out": nrm(ks[2], (DEPTH, MIX_WIDTH, D_MODEL), BETA * MIX_WIDTH ** -0.5),
        "lam_qk": nrm(ks[3], (DEPTH, 4, DIFF_HEAD_DIM), 0.1),
        "subln_g": 1.0 + nrm(ks[4], (DEPTH, DIFF_V_DIM), 0.02),
        "ssm_lam_re": -0.5 + nrm(ks[5], (DEPTH, G, P), 0.01),
        "ssm_lam_im": jnp.pi * n_idx + nrm(ks[6], (DEPTH, G, P), 0.01),
        "ssm_log_dt": jax.random.uniform(ks[7], (DEPTH, G), f32, math.log(DT_MIN), math.log(DT_MAX)),
        "ssm_b_re": nrm(ks[8], (DEPTH, G, P, C), (2.0 * C) ** -0.5),
        "ssm_b_im": nrm(ks[9], (DEPTH, G, P, C), (2.0 * C) ** -0.5),
        "ssm_c_re": nrm(ks[10], (DEPTH, G, C, P), 2.0 ** -0.5),
        "ssm_c_im": nrm(ks[11], (DEPTH, G, C, P), 2.0 ** -0.5),
        "ssm_d": nrm(ks[12], (DEPTH, G, C), 1.0),
        "glu_w": nrm(ks[13], (DEPTH, SSM_WIDTH, SSM_WIDTH), SSM_WIDTH ** -0.5),
        "glu_b": nrm(ks[14], (DEPTH, SSM_WIDTH), 0.01),
        "ln1_g": 1.0 + nrm(ks[15], (DEPTH, D_MODEL), 0.02),
        "ln1_b": nrm(ks[16], (DEPTH, D_MODEL), 0.02),
        "w_ff1": nrm(ks[17], (DEPTH, D_MODEL, D_FF), D_MODEL ** -0.5),
        "w_ff2": nrm(ks[18], (DEPTH, D_FF, D_MODEL), BETA * D_FF ** -0.5),
        "ln2_g": 1.0 + nrm(ks[19], (DEPTH, D_MODEL), 0.02),
        "ln2_b": nrm(ks[20], (DEPTH, D_MODEL), 0.02),
    }


def reference(x, w_in, w_out, lam_qk, subln_g, ssm_lam_re, ssm_lam_im, ssm_log_dt,
              ssm_b_re, ssm_b_im, ssm_c_re, ssm_c_im, ssm_d, glu_w, glu_b,
              ln1_g, ln1_b, w_ff1, w_ff2, ln2_g, ln2_b):
    seq_len = x.shape[1]
    cos, sin = rope_tables(seq_len)
    splits = [QK_WIDTH, 2 * QK_WIDTH, 2 * QK_WIDTH + V_WIDTH]
    for l in range(DEPTH):
        lam_init = 0.8 - 0.6 * math.exp(-0.3 * l)
        lq = lam_qk[l].astype(jnp.float32)
        lam = jnp.exp(jnp.sum(lq[0] * lq[1])) - jnp.exp(jnp.sum(lq[2] * lq[3])) + lam_init
        h = x @ w_in[l]
        q, k, v, u = jnp.split(h, splits, axis=-1)
        attn_out = diff_attention(q, k, v, lam, lam_init, subln_g[l], cos, sin)
        ssm_out = s5_ssm(u, ssm_lam_re[l], ssm_lam_im[l], ssm_log_dt[l], ssm_b_re[l], ssm_b_im[l],
                         ssm_c_re[l], ssm_c_im[l], ssm_d[l], glu_w[l], glu_b[l])
        mix = jnp.concatenate([attn_out, ssm_out], axis=-1) @ w_out[l]
        x = layer_norm(ALPHA * x + mix, ln1_g[l], ln1_b[l])
        ff = jnp.square(jax.nn.relu(x @ w_ff1[l])) @ w_ff2[l]
        x = layer_norm(ALPHA * x + ff, ln2_g[l], ln2_b[l])
    return x
```

```python
import functools
import math

import jax
import jax.numpy as jnp
from jax import lax
from jax.experimental import pallas as pl
from jax.experimental.pallas import tpu as pltpu

F32 = jnp.float32
BF16 = jnp.bfloat16

HEAD_DIM = 64
V_DIM = 2 * HEAD_DIM
SSM_GROUP = 16
SSM_STATE = 64
ROPE_THETA = 10000.0
LN_EPS = 1e-5
RMS_EPS = 1e-5
LANES = 128
SUBLANES = 8
NEG_BIG = -1e30
VMEM_LIMIT = 56 * 1024 * 1024


def _dot(a, b):
    return jnp.dot(a, b, preferred_element_type=F32)


def _layer_norm(y, g, b):
    mu = jnp.mean(y, axis=-1, keepdims=True)
    d = y - mu
    var = jnp.mean(d * d, axis=-1, keepdims=True)
    return d * lax.rsqrt(var + LN_EPS) * g + b


def _in_proj_kernel(x_ref, wqk_ref, wrot_ref, wv_ref, wu_ref, cos_ref, sin_ref,
                    qkv_ref, u_ref, *, qk_width):
    x = x_ref[...].astype(BF16)
    cos = cos_ref[...]
    sin = sin_ref[...]
    half = qk_width
    for c in range(2):
        hq = _dot(x, wqk_ref[:, c * half:(c + 1) * half])
        hr = _dot(x, wrot_ref[:, c * half:(c + 1) * half])
        for j in range(half // LANES):
            sl = slice(j * LANES, (j + 1) * LANES)
            qkv_ref[:, c * half + j * LANES:c * half + (j + 1) * LANES] = (
                hq[:, sl] * cos + hr[:, sl] * sin).astype(BF16)
    qkv_ref[:, 2 * half:] = _dot(x, wv_ref[...]).astype(BF16)
    u_ref[...] = _dot(x, wu_ref[...])


def _in_proj(x, wqk, wrot, wv, wu, cos, sin, *, tl):
    bsz, seq, d = x.shape
    qk2 = wqk.shape[1]
    vw = wv.shape[1]
    uw = wu.shape[1]
    qkv_w = qk2 + vw
    const = lambda b, t: (0, 0)
    return pl.pallas_call(
        functools.partial(_in_proj_kernel, qk_width=qk2 // 2),
        grid=(bsz, seq // tl),
        in_specs=[
            pl.BlockSpec((None, tl, d), lambda b, t: (b, t, 0)),
            pl.BlockSpec(wqk.shape, const),
            pl.BlockSpec(wrot.shape, const),
            pl.BlockSpec(wv.shape, const),
            pl.BlockSpec(wu.shape, const),
            pl.BlockSpec((tl, LANES), lambda b, t: (t, 0)),
            pl.BlockSpec((tl, LANES), lambda b, t: (t, 0)),
        ],
        out_specs=[
            pl.BlockSpec((tl, qkv_w), lambda b, t: (t, b)),
            pl.BlockSpec((tl, uw), lambda b, t: (t, b)),
        ],
        out_shape=[
            jax.ShapeDtypeStruct((seq, bsz * qkv_w), BF16),
            jax.ShapeDtypeStruct((seq, bsz * uw), F32),
        ],
        compiler_params=pltpu.CompilerParams(
            dimension_semantics=("parallel", "parallel"), vmem_limit_bytes=VMEM_LIMIT),
        name="in_proj",
    )(x, wqk, wrot, wv, wu, cos, sin)


def _attn_kernel(lq_ref, g_ref, q_ref, k_ref, v_ref, o_ref, m_ref, l_ref, acc_ref,
                 *, tq, lam_init):
    qi = pl.program_id(2)
    q = q_ref[...].astype(F32)
    lane = lax.broadcasted_iota(jnp.int32, q.shape, 1)
    qm = (jnp.where(lane < HEAD_DIM, q, 0.0).astype(BF16),
          jnp.where(lane >= HEAD_DIM, q, 0.0).astype(BF16))
    m_ref[...] = jnp.full(m_ref.shape, NEG_BIG, F32)
    l_ref[...] = jnp.zeros(l_ref.shape, F32)
    acc_ref[...] = jnp.zeros(acc_ref.shape, F32)

    def step(kt, masked):
        k0 = pl.multiple_of(kt * tq, tq)
        k = k_ref[pl.ds(k0, tq), :]
        v = v_ref[pl.ds(k0, tq), :]
        for mp in range(2):
            s = lax.dot_general(qm[mp], k, (((1,), (1,)), ((), ())),
                                preferred_element_type=F32)
            if masked:
                row = lax.broadcasted_iota(jnp.int32, s.shape, 0)
                col = lax.broadcasted_iota(jnp.int32, s.shape, 1)
                s = jnp.where(col <= row, s, NEG_BIG)
            m_prev = m_ref[mp]
            m_new = jnp.maximum(m_prev, jnp.max(s, axis=1, keepdims=True))
            alpha = jnp.exp(m_prev - m_new)
            p = jnp.exp(s - m_new)
            l_ref[mp] = alpha * l_ref[mp] + jnp.sum(p, axis=1, keepdims=True)
            acc_ref[mp] = alpha * acc_ref[mp] + _dot(p.astype(BF16), v)
            m_ref[mp] = m_new

    def body(kt, carry):
        step(kt, False)
        return carry

    lax.fori_loop(0, qi, body, 0)
    step(qi, True)

    lq = lq_ref[...]
    lam = (jnp.exp(jnp.sum(lq[0:1] * lq[1:2], axis=1, keepdims=True))
           - jnp.exp(jnp.sum(lq[2:3] * lq[3:4], axis=1, keepdims=True)) + lam_init)
    o = acc_ref[0] / l_ref[0] - lam * (acc_ref[1] / l_ref[1])
    ms = jnp.mean(o * o, axis=1, keepdims=True)
    o = o * lax.rsqrt(ms + RMS_EPS) * g_ref[...] * (1.0 - lam_init)
    o_ref[...] = o.astype(o_ref.dtype)


def _attention(qkv, lam_qk, subln_g, *, bsz, n_heads, lam_init, tq):
    seq = qkv.shape[0]
    nblk = 3 * n_heads
    return pl.pallas_call(
        functools.partial(_attn_kernel, tq=tq, lam_init=lam_init),
        grid=(bsz, n_heads, seq // tq),
        in_specs=[
            pl.BlockSpec(lam_qk.shape, lambda b, h, i: (0, 0)),
            pl.BlockSpec((1, V_DIM), lambda b, h, i: (0, 0)),
            pl.BlockSpec((tq, V_DIM), lambda b, h, i: (i, b * nblk + h)),
            pl.BlockSpec((seq, V_DIM), lambda b, h, i: (0, b * nblk + n_heads + h)),
            pl.BlockSpec((seq, V_DIM), lambda b, h, i: (0, b * nblk + 2 * n_heads + h)),
        ],
        out_specs=pl.BlockSpec((tq, V_DIM), lambda b, h, i: (i, b * n_heads + h)),
        out_shape=jax.ShapeDtypeStruct((seq, bsz * n_heads * V_DIM), BF16),
        scratch_shapes=[
            pltpu.VMEM((2, tq, 1), F32),
            pltpu.VMEM((2, tq, 1), F32),
            pltpu.VMEM((2, tq, V_DIM), F32),
        ],
        compiler_params=pltpu.CompilerParams(
            dimension_semantics=("parallel", "parallel", "arbitrary"),
            vmem_limit_bytes=VMEM_LIMIT),
        name="diff_attn",
    )(lam_qk, subln_g.reshape(1, V_DIM), qkv, qkv, qkv)


def _gelu_tanh(y):
    c = math.sqrt(2.0 / math.pi)
    return 0.5 * y * (1.0 + jnp.tanh(c * (y + 0.044715 * (y * y * y))))


def _ssm_kernel(u_ref, bmat_ref, cmat_ref, ar_ref, ai_ref, d_ref, gw_ref, gb_ref,
                out_ref, bu_ref, st_ref, *, tc, slab):
    n_half = bmat_ref.shape[0]
    uk = bmat_ref.shape[1]
    hw = bmat_ref.shape[2]
    cw = hw // 2

    @pl.when(pl.program_id(0) == 0)
    def _():
        st_ref[...] = jnp.zeros(st_ref.shape, F32)

    u = u_ref[...]
    ub = u.astype(BF16)
    for hf in range(n_half):
        bu_ref[:, hf * hw:(hf + 1) * hw] = _dot(ub[:, hf * uk:(hf + 1) * uk], bmat_ref[hf])

    for hf in range(n_half):
        for sl in range(cw // slab):
            cr = hf * hw + sl * slab
            ci = cr + cw
            ca = hf * cw + sl * slab
            a_re = ar_ref[:, ca:ca + slab]
            a_im = ai_ref[:, ca:ca + slab]

            def body(t, carry, cr=cr, ci=ci, a_re=a_re, a_im=a_im):
                s_re, s_im = carry
                r0 = pl.multiple_of(t * SUBLANES, SUBLANES)
                n_re = a_re * s_re - a_im * s_im + bu_ref[pl.ds(r0, SUBLANES), cr:cr + slab]
                n_im = a_re * s_im + a_im * s_re + bu_ref[pl.ds(r0, SUBLANES), ci:ci + slab]
                bu_ref[pl.ds(r0, SUBLANES), cr:cr + slab] = n_re
                bu_ref[pl.ds(r0, SUBLANES), ci:ci + slab] = n_im
                return n_re, n_im

            s_re, s_im = lax.fori_loop(
                0, tc, body, (st_ref[:, cr:cr + slab], st_ref[:, ci:ci + slab]), unroll=8)
            st_ref[:, cr:cr + slab] = s_re
            st_ref[:, ci:ci + slab] = s_im

    ys = []
    for hf in range(n_half):
        s = bu_ref[:, hf * hw:(hf + 1) * hw].astype(BF16)
        ys.append(_dot(s, cmat_ref[hf]))
    y = jnp.concatenate(ys, axis=1) + d_ref[...] * u
    y = _gelu_tanh(y)
    z = _dot(y.astype(BF16), gw_ref[...]) + gb_ref[...]
    out_ref[...] = (y * jax.nn.sigmoid(z)).astype(out_ref.dtype)


def _ssm(u_rows, bmat, cmat, a_re, a_im, d_skip, glu_w, glu_b, *, tc, slab=512):
    rows, width = u_rows.shape
    blk = tc * SUBLANES
    state_w = bmat.shape[0] * bmat.shape[2]
    c2 = lambda t: (0, 0)
    c3 = lambda t: (0, 0, 0)
    return pl.pallas_call(
        functools.partial(_ssm_kernel, tc=tc, slab=slab),
        grid=(rows // blk,),
        in_specs=[
            pl.BlockSpec((blk, width), lambda t: (t, 0)),
            pl.BlockSpec(bmat.shape, c3),
            pl.BlockSpec(cmat.shape, c3),
            pl.BlockSpec(a_re.shape, c2),
            pl.BlockSpec(a_im.shape, c2),
            pl.BlockSpec(d_skip.shape, c2),
            pl.BlockSpec(glu_w.shape, c2),
            pl.BlockSpec(glu_b.shape, c2),
        ],
        out_specs=pl.BlockSpec((blk, width), lambda t: (t, 0)),
        out_shape=jax.ShapeDtypeStruct((rows, width), BF16),
        scratch_shapes=[
            pltpu.VMEM((blk, state_w), F32),
            pltpu.VMEM((SUBLANES, state_w), F32),
        ],
        compiler_params=pltpu.CompilerParams(
            dimension_semantics=("arbitrary",), vmem_limit_bytes=VMEM_LIMIT),
        name="s5_ssm",
    )(u_rows, bmat, cmat, a_re, a_im, d_skip, glu_w, glu_b)


def _out_proj_kernel(x_ref, a_ref, s_ref, wa_ref, ws_ref, g_ref, b_ref, o_ref, *, alpha):
    mix = _dot(a_ref[...], wa_ref[...]) + _dot(s_ref[...], ws_ref[...])
    y = alpha * x_ref[...] + mix
    o_ref[...] = _layer_norm(y, g_ref[...], b_ref[...])


def _out_proj(x, attn, ssm, wa, ws, g, b, *, alpha, tl):
    bsz, seq, d = x.shape
    aw = wa.shape[0]
    sw = ws.shape[0]
    const = lambda bb, t: (0, 0)
    return pl.pallas_call(
        functools.partial(_out_proj_kernel, alpha=alpha),
        grid=(bsz, seq // tl),
        in_specs=[
            pl.BlockSpec((None, tl, d), lambda bb, t: (bb, t, 0)),
            pl.BlockSpec((tl, aw), lambda bb, t: (t, bb)),
            pl.BlockSpec((tl, sw), lambda bb, t: (t, bb)),
            pl.BlockSpec(wa.shape, const),
            pl.BlockSpec(ws.shape, const),
            pl.BlockSpec((1, d), const),
            pl.BlockSpec((1, d), const),
        ],
        out_specs=pl.BlockSpec((None, tl, d), lambda bb, t: (bb, t, 0)),
        out_shape=jax.ShapeDtypeStruct((bsz, seq, d), F32),
        compiler_params=pltpu.CompilerParams(
            dimension_semantics=("parallel", "parallel"), vmem_limit_bytes=VMEM_LIMIT),
        name="out_proj_ln",
    )(x, attn, ssm, wa, ws, g.reshape(1, d), b.reshape(1, d))


def _ffn_kernel(x_ref, w1_ref, w2_ref, g_ref, b_ref, o_ref, acc_ref, *, alpha):
    f = pl.program_id(2)

    @pl.when(f == 0)
    def _():
        acc_ref[...] = jnp.zeros(acc_ref.shape, F32)

    h = jnp.maximum(_dot(x_ref[...].astype(BF16), w1_ref[...]), 0.0)
    acc_ref[...] += _dot((h * h).astype(BF16), w2_ref[...])

    @pl.when(f == pl.num_programs(2) - 1)
    def _():
        y = alpha * x_ref[...] + acc_ref[...]
        o_ref[...] = _layer_norm(y, g_ref[...], b_ref[...])


def _ffn(x, w1, w2, g, b, *, alpha, tl, tf):
    bsz, seq, d = x.shape
    dff = w1.shape[1]
    return pl.pallas_call(
        functools.partial(_ffn_kernel, alpha=alpha),
        grid=(bsz, seq // tl, dff // tf),
        in_specs=[
            pl.BlockSpec((None, tl, d), lambda bb, t, f: (bb, t, 0)),
            pl.BlockSpec((d, tf), lambda bb, t, f: (0, f)),
            pl.BlockSpec((tf, d), lambda bb, t, f: (f, 0)),
            pl.BlockSpec((1, d), lambda bb, t, f: (0, 0)),
            pl.BlockSpec((1, d), lambda bb, t, f: (0, 0)),
        ],
        out_specs=pl.BlockSpec((None, tl, d), lambda bb, t, f: (bb, t, 0)),
        out_shape=jax.ShapeDtypeStruct((bsz, seq, d), F32),
        scratch_shapes=[pltpu.VMEM((tl, d), F32)],
        compiler_params=pltpu.CompilerParams(
            dimension_semantics=("parallel", "parallel", "arbitrary"),
            vmem_limit_bytes=VMEM_LIMIT),
        name="ffn_ln",
    )(x, w1, w2, g.reshape(1, d), b.reshape(1, d))


def _rope_tables(seq):
    pos = jnp.arange(seq, dtype=F32)
    inv_freq = ROPE_THETA ** (-jnp.arange(0, HEAD_DIM, 2, dtype=F32) / HEAD_DIM)
    ang = pos[:, None] * inv_freq[None, :]
    reps = LANES // (HEAD_DIM // 2)
    return jnp.tile(jnp.cos(ang), (1, reps)), jnp.tile(jnp.sin(ang), (1, reps))


def _rotate_half_columns(w):
    d, n = w.shape
    w4 = w.reshape(d, n // HEAD_DIM, 2, HEAD_DIM // 2)
    return jnp.stack([-w4[:, :, 1], w4[:, :, 0]], axis=2).reshape(d, n)


def _ssm_params(lam_re, lam_im, log_dt, b_re, b_im, c_re, c_im, n_half=2):
    g, p = lam_re.shape
    c = b_re.shape[-1]
    lam_c = lax.complex(lam_re.astype(F32), lam_im.astype(F32))
    dt = jnp.exp(log_dt.astype(F32))[:, None]
    a_bar = jnp.exp(lam_c * dt)
    b_bar = ((a_bar - 1.0) / lam_c)[..., None] * lax.complex(b_re.astype(F32), b_im.astype(F32))
    gh = g // n_half
    eye = jnp.eye(gh, dtype=F32)

    def blockdiag_in(m):
        m = m.reshape(n_half, gh, p, c)
        return jnp.einsum('hgpc,gk->hgckp', m, eye).reshape(n_half, gh * c, gh * p)

    def blockdiag_out(m):
        m = m.reshape(n_half, gh, c, p)
        return jnp.einsum('hgcp,gk->hgpkc', m, eye).reshape(n_half, gh * p, gh * c)

    bmat = jnp.concatenate([blockdiag_in(jnp.real(b_bar)), blockdiag_in(jnp.imag(b_bar))], axis=2)
    cmat = jnp.concatenate([blockdiag_out(c_re.astype(F32)), blockdiag_out(-c_im.astype(F32))], axis=1)
    a_re = jnp.broadcast_to(jnp.real(a_bar).reshape(1, g * p), (SUBLANES, g * p))
    a_im = jnp.broadcast_to(jnp.imag(a_bar).reshape(1, g * p), (SUBLANES, g * p))
    return bmat.astype(BF16), cmat.astype(BF16), a_re, a_im


def kernel(x, w_in, w_out, lam_qk, subln_g, ssm_lam_re, ssm_lam_im, ssm_log_dt, ssm_b_re, ssm_b_im,
           ssm_c_re, ssm_c_im, ssm_d, glu_w, glu_b, ln1_g, ln1_b, w_ff1, w_ff2, ln2_g, ln2_b):
    bsz, seq, d_model = x.shape
    depth = w_in.shape[0]
    ssm_w = glu_w.shape[1]
    attn_w = w_out.shape[1] - ssm_w
    n_heads = attn_w // V_DIM
    qk_w = n_heads * 2 * HEAD_DIM
    assert bsz == SUBLANES, "the S5 scan keeps the batch on the sublane axis"
    assert w_in.shape[2] == 2 * qk_w + attn_w + ssm_w
    alpha = (2.0 * depth) ** 0.25
    scale = 1.0 / math.sqrt(HEAD_DIM)

    tl = min(512, seq)
    tq = min(512, seq)
    tc = min(64, seq)
    tf = 1024

    cos, sin = _rope_tables(seq)
    for l in range(depth):
        lam_init = 0.8 - 0.6 * math.exp(-0.3 * l)
        w = w_in[l].astype(F32)
        wqk = jnp.concatenate([w[:, :qk_w] * scale, w[:, qk_w:2 * qk_w]], axis=1)
        wrot = _rotate_half_columns(wqk).astype(BF16)
        wqk = wqk.astype(BF16)
        wv = w[:, 2 * qk_w:2 * qk_w + attn_w].astype(BF16)
        wu = w[:, 2 * qk_w + attn_w:].astype(BF16)
        qkv, u = _in_proj(x, wqk, wrot, wv, wu, cos, sin, tl=tl)

        attn = _attention(qkv, lam_qk[l].astype(F32), subln_g[l].astype(F32),
                          bsz=bsz, n_heads=n_heads, lam_init=lam_init, tq=tq)

        bmat, cmat, a_re, a_im = _ssm_params(ssm_lam_re[l], ssm_lam_im[l], ssm_log_dt[l],
                                             ssm_b_re[l], ssm_b_im[l], ssm_c_re[l], ssm_c_im[l])
        ssm = _ssm(u.reshape(seq * bsz, ssm_w), bmat, cmat, a_re, a_im,
                   ssm_d[l].astype(F32).reshape(1, ssm_w), glu_w[l].astype(BF16),
                   glu_b[l].astype(F32).reshape(1, ssm_w), tc=tc)
        ssm = ssm.reshape(seq, bsz * ssm_w)

        wo = w_out[l].astype(BF16)
        x = _out_proj(x, attn, ssm, wo[:attn_w], wo[attn_w:], ln1_g[l].astype(F32),
                      ln1_b[l].astype(F32), alpha=alpha, tl=tl)
        x = _ffn(x, w_ff1[l].astype(BF16), w_ff2[l].astype(BF16), ln2_g[l].astype(F32),
                 ln2_b[l].astype(F32), alpha=alpha, tl=tl, tf=tf)
    return x
```

```python
import functools
import math

import jax
import jax.numpy as jnp
from jax import lax
from jax.experimental import pallas as pl
from jax.experimental.pallas import tpu as pltpu

F32 = jnp.float32
BF16 = jnp.bfloat16

HEAD_DIM = 64
V_DIM = 2 * HEAD_DIM
SSM_GROUP = 16
SSM_STATE = 64
ROPE_THETA = 10000.0
LN_EPS = 1e-5
RMS_EPS = 1e-5
LANES = 128
SUBLANES = 8
NEG_BIG = -1e30
VMEM_LIMIT = 56 * 1024 * 1024


def _dot(a, b):
    return jnp.dot(a, b, preferred_element_type=F32)


def _layer_norm(y, g, b):
    mu = jnp.mean(y, axis=-1, keepdims=True)
    d = y - mu
    var = jnp.mean(d * d, axis=-1, keepdims=True)
    return d * lax.rsqrt(var + LN_EPS) * g + b


_NT = (((1,), (1,)), ((), ()))


def _in_proj_kernel(x_ref, wqt_ref, wqrt_ref, wk_ref, wkr_ref, wvt_ref, wu_ref,
                    cos_ref, sin_ref, cost_ref, sint_ref, qt_ref, k_ref, vt_ref, u_ref):
    x = x_ref[...].astype(BF16)
    hq = lax.dot_general(wqt_ref[...], x, _NT, preferred_element_type=F32)
    hr = lax.dot_general(wqrt_ref[...], x, _NT, preferred_element_type=F32)
    cost = cost_ref[...]
    sint = sint_ref[...]
    for j in range(hq.shape[0] // LANES):
        sl = slice(j * LANES, (j + 1) * LANES)
        qt_ref[sl, :] = (hq[sl] * cost + hr[sl] * sint).astype(BF16)
    vt_ref[...] = lax.dot_general(wvt_ref[...], x, _NT, preferred_element_type=F32).astype(BF16)
    cos = cos_ref[...]
    sin = sin_ref[...]
    hk = _dot(x, wk_ref[...])
    hkr = _dot(x, wkr_ref[...])
    for j in range(hk.shape[1] // LANES):
        sl = slice(j * LANES, (j + 1) * LANES)
        k_ref[:, sl] = (hk[:, sl] * cos + hkr[:, sl] * sin).astype(BF16)
    u_ref[...] = _dot(x, wu_ref[...])


def _in_proj(x, wqt, wqrt, wk, wkr, wvt, wu, cos, sin, cost, sint, *, tl):
    bsz, seq, d = x.shape
    qw = wqt.shape[0]
    kw = wk.shape[1]
    vw = wvt.shape[0]
    uw = wu.shape[1]
    const = lambda b, t: (0, 0)
    return pl.pallas_call(
        _in_proj_kernel,
        grid=(bsz, seq // tl),
        in_specs=[
            pl.BlockSpec((None, tl, d), lambda b, t: (b, t, 0)),
            pl.BlockSpec(wqt.shape, const),
            pl.BlockSpec(wqrt.shape, const),
            pl.BlockSpec(wk.shape, const),
            pl.BlockSpec(wkr.shape, const),
            pl.BlockSpec(wvt.shape, const),
            pl.BlockSpec(wu.shape, const),
            pl.BlockSpec((tl, LANES), lambda b, t: (t, 0)),
            pl.BlockSpec((tl, LANES), lambda b, t: (t, 0)),
            pl.BlockSpec((LANES, tl), lambda b, t: (0, t)),
            pl.BlockSpec((LANES, tl), lambda b, t: (0, t)),
        ],
        out_specs=[
            pl.BlockSpec((qw, tl), lambda b, t: (b, t)),
            pl.BlockSpec((tl, kw), lambda b, t: (t, b)),
            pl.BlockSpec((None, vw, tl), lambda b, t: (t, b, 0)),
            pl.BlockSpec((tl, uw), lambda b, t: (t, b)),
        ],
        out_shape=[
            jax.ShapeDtypeStruct((bsz * qw, seq), BF16),
            jax.ShapeDtypeStruct((seq, bsz * kw), BF16),
            jax.ShapeDtypeStruct((seq // tl, bsz * vw, tl), BF16),
            jax.ShapeDtypeStruct((seq, bsz * uw), F32),
        ],
        compiler_params=pltpu.CompilerParams(
            dimension_semantics=("parallel", "parallel"), vmem_limit_bytes=VMEM_LIMIT),
        name="in_proj",
    )(x, wqt, wqrt, wk, wkr, wvt, wu, cos, sin, cost, sint)


def _attn_kernel(lq_ref, g_ref, qt_ref, k_ref, vt_ref, o_ref, m_ref, l_ref, acc_ref,
                 *, tq, lam_init):
    qi = pl.program_id(2)
    qt = qt_ref[...].astype(F32)
    rowid = lax.broadcasted_iota(jnp.int32, qt.shape, 0)
    qm = (jnp.where(rowid < HEAD_DIM, qt, 0.0).astype(BF16),
          jnp.where(rowid >= HEAD_DIM, qt, 0.0).astype(BF16))
    m_ref[...] = jnp.full(m_ref.shape, NEG_BIG, F32)
    l_ref[...] = jnp.zeros(l_ref.shape, F32)
    acc_ref[...] = jnp.zeros(acc_ref.shape, F32)

    def step(kt, masked):
        k0 = pl.multiple_of(kt * tq, tq)
        k = k_ref[pl.ds(k0, tq), :]
        vt = vt_ref[kt]
        for mp in range(2):
            s = _dot(k, qm[mp])
            if masked:
                row = lax.broadcasted_iota(jnp.int32, s.shape, 0)
                col = lax.broadcasted_iota(jnp.int32, s.shape, 1)
                s = jnp.where(row <= col, s, NEG_BIG)
            s3 = s.reshape(tq // SUBLANES, SUBLANES, tq)
            m_prev = m_ref[mp]
            m_new = jnp.maximum(m_prev, jnp.max(jnp.max(s3, axis=0), axis=0, keepdims=True))
            alpha = jnp.exp2(m_prev - m_new)
            p = jnp.exp2(s - m_new)
            l_ref[mp] = alpha * l_ref[mp] + jnp.sum(p.reshape(s3.shape), axis=0)
            acc_ref[mp] = alpha * acc_ref[mp] + _dot(vt, p.astype(BF16))
            m_ref[mp] = m_new

    def body(kt, carry):
        step(kt, False)
        return carry

    lax.fori_loop(0, qi, body, 0)
    step(qi, True)

    lq = lq_ref[...]
    lam = (jnp.exp(jnp.sum(lq[0:1] * lq[1:2], axis=1, keepdims=True))
           - jnp.exp(jnp.sum(lq[2:3] * lq[3:4], axis=1, keepdims=True)) + lam_init)
    l0 = jnp.sum(l_ref[0], axis=0, keepdims=True)
    l1 = jnp.sum(l_ref[1], axis=0, keepdims=True)
    ot = acc_ref[0] / l0 - lam * (acc_ref[1] / l1)
    ms = jnp.mean(ot * ot, axis=0, keepdims=True)
    ot = ot * lax.rsqrt(ms + RMS_EPS) * g_ref[...] * (1.0 - lam_init)
    o_ref[...] = ot.T.astype(o_ref.dtype)


def _attention(qt, k, vt, lam_qk, subln_g, *, bsz, n_heads, lam_init, tq):
    seq = k.shape[0]
    return pl.pallas_call(
        functools.partial(_attn_kernel, tq=tq, lam_init=lam_init),
        grid=(bsz, n_heads, seq // tq),
        in_specs=[
            pl.BlockSpec(lam_qk.shape, lambda b, h, i: (0, 0)),
            pl.BlockSpec((V_DIM, 1), lambda b, h, i: (0, 0)),
            pl.BlockSpec((V_DIM, tq), lambda b, h, i: (b * n_heads + h, i)),
            pl.BlockSpec((seq, V_DIM), lambda b, h, i: (0, b * n_heads + h)),
            pl.BlockSpec((seq // tq, V_DIM, tq), lambda b, h, i: (0, b * n_heads + h, 0)),
        ],
        out_specs=pl.BlockSpec((tq, V_DIM), lambda b, h, i: (i, b * n_heads + h)),
        out_shape=jax.ShapeDtypeStruct((seq, bsz * n_heads * V_DIM), BF16),
        scratch_shapes=[
            pltpu.VMEM((2, 1, tq), F32),
            pltpu.VMEM((2, SUBLANES, tq), F32),
            pltpu.VMEM((2, V_DIM, tq), F32),
        ],
        compiler_params=pltpu.CompilerParams(
            dimension_semantics=("parallel", "parallel", "arbitrary"),
            vmem_limit_bytes=VMEM_LIMIT),
        name="diff_attn",
    )(lam_qk, subln_g.reshape(V_DIM, 1), qt, k, vt)


def _gelu_tanh(y):
    c = math.sqrt(2.0 / math.pi)
    return 0.5 * y * (1.0 + jnp.tanh(c * (y + 0.044715 * (y * y * y))))


def _ssm_kernel(u_ref, bmat_ref, cmat_ref, ar_ref, ai_ref, d_ref, gw_ref, gb_ref,
                out_ref, bu_ref, st_ref, *, tc, slab):
    n_half = bmat_ref.shape[0]
    uk = bmat_ref.shape[1]
    hw = bmat_ref.shape[2]
    cw = hw // 2

    @pl.when(pl.program_id(0) == 0)
    def _():
        st_ref[...] = jnp.zeros(st_ref.shape, F32)

    u = u_ref[...]
    ub = u.astype(BF16)
    for hf in range(n_half):
        bu_ref[:, hf * hw:(hf + 1) * hw] = _dot(ub[:, hf * uk:(hf + 1) * uk], bmat_ref[hf])

    for hf in range(n_half):
        for sl in range(cw // slab):
            cr = hf * hw + sl * slab
            ci = cr + cw
            ca = hf * cw + sl * slab
            a_re = ar_ref[:, ca:ca + slab]
            a_im = ai_ref[:, ca:ca + slab]

            def body(t, carry, cr=cr, ci=ci, a_re=a_re, a_im=a_im):
                s_re, s_im = carry
                r0 = pl.multiple_of(t * SUBLANES, SUBLANES)
                n_re = a_re * s_re - a_im * s_im + bu_ref[pl.ds(r0, SUBLANES), cr:cr + slab]
                n_im = a_re * s_im + a_im * s_re + bu_ref[pl.ds(r0, SUBLANES), ci:ci + slab]
                bu_ref[pl.ds(r0, SUBLANES), cr:cr + slab] = n_re
                bu_ref[pl.ds(r0, SUBLANES), ci:ci + slab] = n_im
                return n_re, n_im

            s_re, s_im = lax.fori_loop(
                0, tc, body, (st_ref[:, cr:cr + slab], st_ref[:, ci:ci + slab]), unroll=8)
            st_ref[:, cr:cr + slab] = s_re
            st_ref[:, ci:ci + slab] = s_im

    ys = []
    for hf in range(n_half):
        s = bu_ref[:, hf * hw:(hf + 1) * hw].astype(BF16)
        ys.append(_dot(s, cmat_ref[hf]))
    y = jnp.concatenate(ys, axis=1) + d_ref[...] * u
    y = _gelu_tanh(y)
    z = _dot(y.astype(BF16), gw_ref[...]) + gb_ref[...]
    out_ref[...] = (y * jax.nn.sigmoid(z)).astype(out_ref.dtype)


def _ssm(u_rows, bmat, cmat, a_re, a_im, d_skip, glu_w, glu_b, *, tc, slab=512):
    rows, width = u_rows.shape
    blk = tc * SUBLANES
    state_w = bmat.shape[0] * bmat.shape[2]
    c2 = lambda t: (0, 0)
    c3 = lambda t: (0, 0, 0)
    return pl.pallas_call(
        functools.partial(_ssm_kernel, tc=tc, slab=slab),
        grid=(rows // blk,),
        in_specs=[
            pl.BlockSpec((blk, width), lambda t: (t, 0)),
            pl.BlockSpec(bmat.shape, c3),
            pl.BlockSpec(cmat.shape, c3),
            pl.BlockSpec(a_re.shape, c2),
            pl.BlockSpec(a_im.shape, c2),
            pl.BlockSpec(d_skip.shape, c2),
            pl.BlockSpec(glu_w.shape, c2),
            pl.BlockSpec(glu_b.shape, c2),
        ],
        out_specs=pl.BlockSpec((blk, width), lambda t: (t, 0)),
        out_shape=jax.ShapeDtypeStruct((rows, width), BF16),
        scratch_shapes=[
            pltpu.VMEM((blk, state_w), F32),
            pltpu.VMEM((SUBLANES, state_w), F32),
        ],
        compiler_params=pltpu.CompilerParams(
            dimension_semantics=("arbitrary",), vmem_limit_bytes=VMEM_LIMIT),
        name="s5_ssm",
    )(u_rows, bmat, cmat, a_re, a_im, d_skip, glu_w, glu_b)


def _out_proj_kernel(x_ref, a_ref, s_ref, wa_ref, ws_ref, g_ref, b_ref, o_ref, *, alpha):
    mix = _dot(a_ref[...], wa_ref[...]) + _dot(s_ref[...], ws_ref[...])
    y = alpha * x_ref[...] + mix
    o_ref[...] = _layer_norm(y, g_ref[...], b_ref[...])


def _out_proj(x, attn, ssm, wa, ws, g, b, *, alpha, tl):
    bsz, seq, d = x.shape
    aw = wa.shape[0]
    sw = ws.shape[0]
    const = lambda bb, t: (0, 0)
    return pl.pallas_call(
        functools.partial(_out_proj_kernel, alpha=alpha),
        grid=(bsz, seq // tl),
        in_specs=[
            pl.BlockSpec((None, tl, d), lambda bb, t: (bb, t, 0)),
            pl.BlockSpec((tl, aw), lambda bb, t: (t, bb)),
            pl.BlockSpec((tl, sw), lambda bb, t: (t, bb)),
            pl.BlockSpec(wa.shape, const),
            pl.BlockSpec(ws.shape, const),
            pl.BlockSpec((1, d), const),
            pl.BlockSpec((1, d), const),
        ],
        out_specs=pl.BlockSpec((None, tl, d), lambda bb, t: (bb, t, 0)),
        out_shape=jax.ShapeDtypeStruct((bsz, seq, d), F32),
        compiler_params=pltpu.CompilerParams(
            dimension_semantics=("parallel", "parallel"), vmem_limit_bytes=VMEM_LIMIT),
        name="out_proj_ln",
    )(x, attn, ssm, wa, ws, g.reshape(1, d), b.reshape(1, d))


def _ffn_kernel(x_ref, w1_ref, w2_ref, g_ref, b_ref, o_ref, acc_ref, *, alpha):
    f = pl.program_id(2)

    @pl.when(f == 0)
    def _():
        acc_ref[...] = jnp.zeros(acc_ref.shape, F32)

    h = jnp.maximum(_dot(x_ref[...].astype(BF16), w1_ref[...]), 0.0)
    acc_ref[...] += _dot((h * h).astype(BF16), w2_ref[...])

    @pl.when(f == pl.num_programs(2) - 1)
    def _():
        y = alpha * x_ref[...] + acc_ref[...]
        o_ref[...] = _layer_norm(y, g_ref[...], b_ref[...])


def _ffn(x, w1, w2, g, b, *, alpha, tl, tf):
    bsz, seq, d = x.shape
    dff = w1.shape[1]
    return pl.pallas_call(
        functools.partial(_ffn_kernel, alpha=alpha),
        grid=(bsz, seq // tl, dff // tf),
        in_specs=[
            pl.BlockSpec((None, tl, d), lambda bb, t, f: (bb, t, 0)),
            pl.BlockSpec((d, tf), lambda bb, t, f: (0, f)),
            pl.BlockSpec((tf, d), lambda bb, t, f: (f, 0)),
            pl.BlockSpec((1, d), lambda bb, t, f: (0, 0)),
            pl.BlockSpec((1, d), lambda bb, t, f: (0, 0)),
        ],
        out_specs=pl.BlockSpec((None, tl, d), lambda bb, t, f: (bb, t, 0)),
        out_shape=jax.ShapeDtypeStruct((bsz, seq, d), F32),
        scratch_shapes=[pltpu.VMEM((tl, d), F32)],
        compiler_params=pltpu.CompilerParams(
            dimension_semantics=("parallel", "parallel", "arbitrary"),
            vmem_limit_bytes=VMEM_LIMIT),
        name="ffn_ln",
    )(x, w1, w2, g.reshape(1, d), b.reshape(1, d))


def _rope_tables(seq, q_scale):
    pos = jnp.arange(seq, dtype=F32)
    inv_freq = ROPE_THETA ** (-jnp.arange(0, HEAD_DIM, 2, dtype=F32) / HEAD_DIM)
    ang = pos[:, None] * inv_freq[None, :]
    reps = LANES // (HEAD_DIM // 2)
    cos = jnp.tile(jnp.cos(ang), (1, reps))
    sin = jnp.tile(jnp.sin(ang), (1, reps))
    return cos, sin, (cos * q_scale).T, (sin * q_scale).T


def _rotate_half_columns(w):
    d, n = w.shape
    w4 = w.reshape(d, n // HEAD_DIM, 2, HEAD_DIM // 2)
    return jnp.stack([-w4[:, :, 1], w4[:, :, 0]], axis=2).reshape(d, n)


def _ssm_params(lam_re, lam_im, log_dt, b_re, b_im, c_re, c_im, n_half=2):
    g, p = lam_re.shape
    c = b_re.shape[-1]
    lr = lam_re.astype(F32)
    li = lam_im.astype(F32)
    dt = jnp.exp(log_dt.astype(F32))[:, None]
    mag = jnp.exp(lr * dt)
    ar = mag * jnp.cos(li * dt)
    ai = mag * jnp.sin(li * dt)
    den = lr * lr + li * li
    fr = ((ar - 1.0) * lr + ai * li) / den
    fi = (ai * lr - (ar - 1.0) * li) / den
    br = b_re.astype(F32)
    bi = b_im.astype(F32)
    bbar_re = fr[..., None] * br - fi[..., None] * bi
    bbar_im = fr[..., None] * bi + fi[..., None] * br
    gh = g // n_half
    eye = jnp.eye(gh, dtype=F32)

    def blockdiag_in(m):
        m = m.reshape(n_half, gh, p, c)
        return jnp.einsum('hgpc,gk->hgckp', m, eye).reshape(n_half, gh * c, gh * p)

    def blockdiag_out(m):
        m = m.reshape(n_half, gh, c, p)
        return jnp.einsum('hgcp,gk->hgpkc', m, eye).reshape(n_half, gh * p, gh * c)

    bmat = jnp.concatenate([blockdiag_in(bbar_re), blockdiag_in(bbar_im)], axis=2)
    cmat = jnp.concatenate([blockdiag_out(c_re.astype(F32)), blockdiag_out(-c_im.astype(F32))], axis=1)
    a_re = jnp.broadcast_to(ar.reshape(1, g * p), (SUBLANES, g * p))
    a_im = jnp.broadcast_to(ai.reshape(1, g * p), (SUBLANES, g * p))
    return bmat.astype(BF16), cmat.astype(BF16), a_re, a_im


def kernel(x, w_in, w_out, lam_qk, subln_g, ssm_lam_re, ssm_lam_im, ssm_log_dt, ssm_b_re, ssm_b_im,
           ssm_c_re, ssm_c_im, ssm_d, glu_w, glu_b, ln1_g, ln1_b, w_ff1, w_ff2, ln2_g, ln2_b):
    bsz, seq, d_model = x.shape
    depth = w_in.shape[0]
    ssm_w = glu_w.shape[1]
    attn_w = w_out.shape[1] - ssm_w
    n_heads = attn_w // V_DIM
    qk_w = n_heads * 2 * HEAD_DIM
    assert bsz == SUBLANES, "the S5 scan keeps the batch on the sublane axis"
    assert w_in.shape[2] == 2 * qk_w + attn_w + ssm_w
    alpha = (2.0 * depth) ** 0.25
    q_scale = math.log2(math.e) / math.sqrt(HEAD_DIM)

    tl = min(512, seq)
    tq = tl
    tc = min(64, seq)
    tf = 1024

    cos, sin, cost, sint = _rope_tables(seq, q_scale)
    for l in range(depth):
        lam_init = 0.8 - 0.6 * math.exp(-0.3 * l)
        w = w_in[l].astype(F32)
        wq = w[:, :qk_w]
        wk = w[:, qk_w:2 * qk_w]
        wv = w[:, 2 * qk_w:2 * qk_w + attn_w]
        wu = w[:, 2 * qk_w + attn_w:]
        qt, k, vt, u = _in_proj(
            x, wq.T.astype(BF16), _rotate_half_columns(wq).T.astype(BF16), wk.astype(BF16),
            _rotate_half_columns(wk).astype(BF16), wv.T.astype(BF16), wu.astype(BF16),
            cos, sin, cost, sint, tl=tl)

        attn = _attention(qt, k, vt, lam_qk[l].astype(F32), subln_g[l].astype(F32),
                          bsz=bsz, n_heads=n_heads, lam_init=lam_init, tq=tq)

        bmat, cmat, a_re, a_im = _ssm_params(ssm_lam_re[l], ssm_lam_im[l], ssm_log_dt[l],
                                             ssm_b_re[l], ssm_b_im[l], ssm_c_re[l], ssm_c_im[l])
        ssm = _ssm(u.reshape(seq * bsz, ssm_w), bmat, cmat, a_re, a_im,
                   ssm_d[l].astype(F32).reshape(1, ssm_w), glu_w[l].astype(BF16),
                   glu_b[l].astype(F32).reshape(1, ssm_w), tc=tc)
        ssm = ssm.reshape(seq, bsz * ssm_w)

        wo = w_out[l].astype(BF16)
        x = _out_proj(x, attn, ssm, wo[:attn_w], wo[attn_w:], ln1_g[l].astype(F32),
                      ln1_b[l].astype(F32), alpha=alpha, tl=tl)
        x = _ffn(x, w_ff1[l].astype(BF16), w_ff2[l].astype(BF16), ln2_g[l].astype(F32),
                 ln2_b[l].astype(F32), alpha=alpha, tl=tl, tf=tf)
    return x
```

```python
import functools
import math

import jax
import jax.numpy as jnp
from jax import lax
from jax.experimental import pallas as pl
from jax.experimental.pallas import tpu as pltpu

F32 = jnp.float32
BF16 = jnp.bfloat16

HEAD_DIM = 64
V_DIM = 2 * HEAD_DIM
SSM_GROUP = 16
SSM_STATE = 64
ROPE_THETA = 10000.0
LN_EPS = 1e-5
RMS_EPS = 1e-5
LANES = 128
SUBLANES = 8
NEG_BIG = -1e30
CHUNK = 128
VMEM_LIMIT = 56 * 1024 * 1024


def _dot(a, b):
    return jnp.dot(a, b, preferred_element_type=F32)


def _layer_norm(y, g, b):
    mu = jnp.mean(y, axis=-1, keepdims=True)
    d = y - mu
    var = jnp.mean(d * d, axis=-1, keepdims=True)
    return d * lax.rsqrt(var + LN_EPS) * g + b


_NT = (((1,), (1,)), ((), ()))


def _in_proj_kernel(x_ref, wqt_ref, wqrt_ref, wk_ref, wkr_ref, wvt_ref, wu_ref,
                    cos_ref, sin_ref, cost_ref, sint_ref, qt_ref, k_ref, vt_ref, u_ref):
    x = x_ref[...].astype(BF16)
    hq = lax.dot_general(wqt_ref[...], x, _NT, preferred_element_type=F32)
    hr = lax.dot_general(wqrt_ref[...], x, _NT, preferred_element_type=F32)
    cost = cost_ref[...]
    sint = sint_ref[...]
    for j in range(hq.shape[0] // LANES):
        sl = slice(j * LANES, (j + 1) * LANES)
        qt_ref[sl, :] = (hq[sl] * cost + hr[sl] * sint).astype(BF16)
    vt_ref[...] = lax.dot_general(wvt_ref[...], x, _NT, preferred_element_type=F32).astype(BF16)
    cos = cos_ref[...]
    sin = sin_ref[...]
    hk = _dot(x, wk_ref[...])
    hkr = _dot(x, wkr_ref[...])
    for j in range(hk.shape[1] // LANES):
        sl = slice(j * LANES, (j + 1) * LANES)
        k_ref[:, sl] = (hk[:, sl] * cos + hkr[:, sl] * sin).astype(BF16)
    u_ref[...] = _dot(x, wu_ref[...])


def _in_proj(x, wqt, wqrt, wk, wkr, wvt, wu, cos, sin, cost, sint, *, tl):
    bsz, seq, d = x.shape
    qw = wqt.shape[0]
    kw = wk.shape[1]
    vw = wvt.shape[0]
    uw = wu.shape[1]
    const = lambda b, t: (0, 0)
    return pl.pallas_call(
        _in_proj_kernel,
        grid=(bsz, seq // tl),
        in_specs=[
            pl.BlockSpec((None, tl, d), lambda b, t: (b, t, 0)),
            pl.BlockSpec(wqt.shape, const),
            pl.BlockSpec(wqrt.shape, const),
            pl.BlockSpec(wk.shape, const),
            pl.BlockSpec(wkr.shape, const),
            pl.BlockSpec(wvt.shape, const),
            pl.BlockSpec(wu.shape, const),
            pl.BlockSpec((tl, LANES), lambda b, t: (t, 0)),
            pl.BlockSpec((tl, LANES), lambda b, t: (t, 0)),
            pl.BlockSpec((LANES, tl), lambda b, t: (0, t)),
            pl.BlockSpec((LANES, tl), lambda b, t: (0, t)),
        ],
        out_specs=[
            pl.BlockSpec((qw, tl), lambda b, t: (b, t)),
            pl.BlockSpec((tl, kw), lambda b, t: (t, b)),
            pl.BlockSpec((None, vw, tl), lambda b, t: (t, b, 0)),
            pl.BlockSpec((tl, uw), lambda b, t: (t, b)),
        ],
        out_shape=[
            jax.ShapeDtypeStruct((bsz * qw, seq), BF16),
            jax.ShapeDtypeStruct((seq, bsz * kw), BF16),
            jax.ShapeDtypeStruct((seq // tl, bsz * vw, tl), BF16),
            jax.ShapeDtypeStruct((seq, bsz * uw), F32),
        ],
        compiler_params=pltpu.CompilerParams(
            dimension_semantics=("parallel", "parallel"), vmem_limit_bytes=VMEM_LIMIT),
        name="in_proj",
    )(x, wqt, wqrt, wk, wkr, wvt, wu, cos, sin, cost, sint)


def _attn_kernel(lq_ref, g_ref, qt_ref, k_ref, vt_ref, o_ref, m_ref, l_ref, acc_ref,
                 s_ref, cm_ref, p_ref, a1_ref, *, tq, lam_init):
    qi = pl.program_id(2)
    qt = qt_ref[...].astype(F32)
    rowid = lax.broadcasted_iota(jnp.int32, qt.shape, 0)
    qm = (jnp.where(rowid < HEAD_DIM, qt, 0.0).astype(BF16),
          jnp.where(rowid >= HEAD_DIM, qt, 0.0).astype(BF16))
    m_ref[...] = jnp.full(m_ref.shape, NEG_BIG, F32)
    l_ref[...] = jnp.zeros(l_ref.shape, F32)
    acc_ref[...] = jnp.zeros(acc_ref.shape, F32)
    p_ref[1] = jnp.zeros(p_ref.shape[1:], BF16)
    a1_ref[...] = jnp.ones(a1_ref.shape, F32)
    n_rg = tq // SUBLANES

    def scores(mp, kt):
        k0 = pl.multiple_of(kt * tq, tq)
        s = _dot(k_ref[pl.ds(k0, tq), :], qm[mp])
        s_ref[mp] = s
        cm_ref[mp] = jnp.max(s.reshape(n_rg, SUBLANES, tq), axis=0)

    def softmax_tile(mp, masked):
        if masked:
            cm = jnp.full((SUBLANES, tq), NEG_BIG, F32)
            for c in range(tq // CHUNK):
                sc = s_ref[mp, c * CHUNK:(c + 1) * CHUNK, :]
                row = lax.broadcasted_iota(jnp.int32, sc.shape, 0) + c * CHUNK
                col = lax.broadcasted_iota(jnp.int32, sc.shape, 1)
                sc = jnp.where(row <= col, sc, NEG_BIG)
                s_ref[mp, c * CHUNK:(c + 1) * CHUNK, :] = sc
                cm = jnp.maximum(cm, jnp.max(sc.reshape(CHUNK // SUBLANES, SUBLANES, tq), axis=0))
        else:
            cm = cm_ref[mp]
        m_prev = m_ref[mp]
        m_new = jnp.maximum(m_prev, jnp.max(cm, axis=0, keepdims=True))
        alpha = jnp.exp2(m_prev - m_new)
        m_ref[mp] = m_new
        lsum = jnp.zeros((SUBLANES, tq), F32)
        for c in range(tq // CHUNK):
            pc = jnp.exp2(s_ref[mp, c * CHUNK:(c + 1) * CHUNK, :] - m_new)
            lsum = lsum + jnp.sum(pc.reshape(CHUNK // SUBLANES, SUBLANES, tq), axis=0)
            p_ref[mp, c * CHUNK:(c + 1) * CHUNK, :] = pc.astype(BF16)
        l_ref[mp] = alpha * l_ref[mp] + lsum
        return alpha

    def step(kt, masked, prefetch):
        acc_ref[1] = a1_ref[...] * acc_ref[1] + _dot(vt_ref[jnp.maximum(kt - 1, 0)], p_ref[1])
        scores(1, kt)
        alpha0 = softmax_tile(0, masked)
        acc_ref[0] = alpha0 * acc_ref[0] + _dot(vt_ref[kt], p_ref[0])
        a1_ref[...] = softmax_tile(1, masked)
        if prefetch:
            scores(0, kt + 1)

    def body(kt, carry):
        step(kt, False, True)
        return carry

    scores(0, 0)
    lax.fori_loop(0, qi, body, 0)
    step(qi, True, False)
    acc_ref[1] = a1_ref[...] * acc_ref[1] + _dot(vt_ref[qi], p_ref[1])

    lq = lq_ref[...]
    lam = (jnp.exp(jnp.sum(lq[0:1] * lq[1:2], axis=1, keepdims=True))
           - jnp.exp(jnp.sum(lq[2:3] * lq[3:4], axis=1, keepdims=True)) + lam_init)
    l0 = jnp.sum(l_ref[0], axis=0, keepdims=True)
    l1 = jnp.sum(l_ref[1], axis=0, keepdims=True)
    ot = acc_ref[0] / l0 - lam * (acc_ref[1] / l1)
    ms = jnp.mean(ot * ot, axis=0, keepdims=True)
    ot = ot * lax.rsqrt(ms + RMS_EPS) * g_ref[...] * (1.0 - lam_init)
    o_ref[...] = ot.T.astype(o_ref.dtype)


def _attention(qt, k, vt, lam_qk, subln_g, *, bsz, n_heads, lam_init, tq):
    seq = k.shape[0]
    return pl.pallas_call(
        functools.partial(_attn_kernel, tq=tq, lam_init=lam_init),
        grid=(bsz, n_heads, seq // tq),
        in_specs=[
            pl.BlockSpec(lam_qk.shape, lambda b, h, i: (0, 0)),
            pl.BlockSpec((V_DIM, 1), lambda b, h, i: (0, 0)),
            pl.BlockSpec((V_DIM, tq), lambda b, h, i: (b * n_heads + h, i)),
            pl.BlockSpec((seq, V_DIM), lambda b, h, i: (0, b * n_heads + h)),
            pl.BlockSpec((seq // tq, V_DIM, tq), lambda b, h, i: (0, b * n_heads + h, 0)),
        ],
        out_specs=pl.BlockSpec((tq, V_DIM), lambda b, h, i: (i, b * n_heads + h)),
        out_shape=jax.ShapeDtypeStruct((seq, bsz * n_heads * V_DIM), BF16),
        scratch_shapes=[
            pltpu.VMEM((2, 1, tq), F32),
            pltpu.VMEM((2, SUBLANES, tq), F32),
            pltpu.VMEM((2, V_DIM, tq), F32),
            pltpu.VMEM((2, tq, tq), F32),
            pltpu.VMEM((2, SUBLANES, tq), F32),
            pltpu.VMEM((2, tq, tq), BF16),
            pltpu.VMEM((1, tq), F32),
        ],
        compiler_params=pltpu.CompilerParams(
            dimension_semantics=("parallel", "parallel", "arbitrary"),
            vmem_limit_bytes=VMEM_LIMIT),
        name="diff_attn",
    )(lam_qk, subln_g.reshape(V_DIM, 1), qt, k, vt)


def _gelu_tanh(y):
    c = math.sqrt(2.0 / math.pi)
    return 0.5 * y * (1.0 + jnp.tanh(c * (y + 0.044715 * (y * y * y))))


def _ssm_kernel(u_ref, bmat_ref, cmat_ref, ar_ref, ai_ref, d_ref, gw_ref, gb_ref,
                out_ref, bu_ref, st_ref, *, tc, slab):
    n_half = bmat_ref.shape[0]
    uk = bmat_ref.shape[1]
    hw = bmat_ref.shape[2]
    cw = hw // 2

    @pl.when(pl.program_id(0) == 0)
    def _():
        st_ref[...] = jnp.zeros(st_ref.shape, F32)

    u = u_ref[...]
    ub = u.astype(BF16)
    for hf in range(n_half):
        bu_ref[:, hf * hw:(hf + 1) * hw] = _dot(ub[:, hf * uk:(hf + 1) * uk], bmat_ref[hf])

    for hf in range(n_half):
        for sl in range(cw // slab):
            cr = hf * hw + sl * slab
            ci = cr + cw
            ca = hf * cw + sl * slab
            a_re = ar_ref[:, ca:ca + slab]
            a_im = ai_ref[:, ca:ca + slab]

            def body(t, carry, cr=cr, ci=ci, a_re=a_re, a_im=a_im):
                s_re, s_im = carry
                r0 = pl.multiple_of(t * SUBLANES, SUBLANES)
                n_re = a_re * s_re - a_im * s_im + bu_ref[pl.ds(r0, SUBLANES), cr:cr + slab]
                n_im = a_re * s_im + a_im * s_re + bu_ref[pl.ds(r0, SUBLANES), ci:ci + slab]
                bu_ref[pl.ds(r0, SUBLANES), cr:cr + slab] = n_re
                bu_ref[pl.ds(r0, SUBLANES), ci:ci + slab] = n_im
                return n_re, n_im

            s_re, s_im = lax.fori_loop(
                0, tc, body, (st_ref[:, cr:cr + slab], st_ref[:, ci:ci + slab]), unroll=8)
            st_ref[:, cr:cr + slab] = s_re
            st_ref[:, ci:ci + slab] = s_im

    ys = []
    for hf in range(n_half):
        s = bu_ref[:, hf * hw:(hf + 1) * hw].astype(BF16)
        ys.append(_dot(s, cmat_ref[hf]))
    y = jnp.concatenate(ys, axis=1) + d_ref[...] * u
    y = _gelu_tanh(y)
    z = _dot(y.astype(BF16), gw_ref[...]) + gb_ref[...]
    out_ref[...] = (y * jax.nn.sigmoid(z)).astype(out_ref.dtype)


def _ssm(u_rows, bmat, cmat, a_re, a_im, d_skip, glu_w, glu_b, *, tc, slab=512):
    rows, width = u_rows.shape
    blk = tc * SUBLANES
    state_w = bmat.shape[0] * bmat.shape[2]
    c2 = lambda t: (0, 0)
    c3 = lambda t: (0, 0, 0)
    return pl.pallas_call(
        functools.partial(_ssm_kernel, tc=tc, slab=slab),
        grid=(rows // blk,),
        in_specs=[
            pl.BlockSpec((blk, width), lambda t: (t, 0)),
            pl.BlockSpec(bmat.shape, c3),
            pl.BlockSpec(cmat.shape, c3),
            pl.BlockSpec(a_re.shape, c2),
            pl.BlockSpec(a_im.shape, c2),
            pl.BlockSpec(d_skip.shape, c2),
            pl.BlockSpec(glu_w.shape, c2),
            pl.BlockSpec(glu_b.shape, c2),
        ],
        out_specs=pl.BlockSpec((blk, width), lambda t: (t, 0)),
        out_shape=jax.ShapeDtypeStruct((rows, width), BF16),
        scratch_shapes=[
            pltpu.VMEM((blk, state_w), F32),
            pltpu.VMEM((SUBLANES, state_w), F32),
        ],
        compiler_params=pltpu.CompilerParams(
            dimension_semantics=("arbitrary",), vmem_limit_bytes=VMEM_LIMIT),
        name="s5_ssm",
    )(u_rows, bmat, cmat, a_re, a_im, d_skip, glu_w, glu_b)


def _out_proj_kernel(x_ref, a_ref, s_ref, wa_ref, ws_ref, g_ref, b_ref, o_ref, *, alpha):
    mix = _dot(a_ref[...], wa_ref[...]) + _dot(s_ref[...], ws_ref[...])
    y = alpha * x_ref[...] + mix
    o_ref[...] = _layer_norm(y, g_ref[...], b_ref[...])


def _out_proj(x, attn, ssm, wa, ws, g, b, *, alpha, tl):
    bsz, seq, d = x.shape
    aw = wa.shape[0]
    sw = ws.shape[0]
    const = lambda bb, t: (0, 0)
    return pl.pallas_call(
        functools.partial(_out_proj_kernel, alpha=alpha),
        grid=(bsz, seq // tl),
        in_specs=[
            pl.BlockSpec((None, tl, d), lambda bb, t: (bb, t, 0)),
            pl.BlockSpec((tl, aw), lambda bb, t: (t, bb)),
            pl.BlockSpec((tl, sw), lambda bb, t: (t, bb)),
            pl.BlockSpec(wa.shape, const),
            pl.BlockSpec(ws.shape, const),
            pl.BlockSpec((1, d), const),
            pl.BlockSpec((1, d), const),
        ],
        out_specs=pl.BlockSpec((None, tl, d), lambda bb, t: (bb, t, 0)),
        out_shape=jax.ShapeDtypeStruct((bsz, seq, d), F32),
        compiler_params=pltpu.CompilerParams(
            dimension_semantics=("parallel", "parallel"), vmem_limit_bytes=VMEM_LIMIT),
        name="out_proj_ln",
    )(x, attn, ssm, wa, ws, g.reshape(1, d), b.reshape(1, d))


def _ffn_kernel(x_ref, w1_ref, w2_ref, g_ref, b_ref, o_ref, acc_ref, *, alpha):
    f = pl.program_id(2)

    @pl.when(f == 0)
    def _():
        acc_ref[...] = jnp.zeros(acc_ref.shape, F32)

    h = jnp.maximum(_dot(x_ref[...].astype(BF16), w1_ref[...]), 0.0)
    acc_ref[...] += _dot((h * h).astype(BF16), w2_ref[...])

    @pl.when(f == pl.num_programs(2) - 1)
    def _():
        y = alpha * x_ref[...] + acc_ref[...]
        o_ref[...] = _layer_norm(y, g_ref[...], b_ref[...])


def _ffn(x, w1, w2, g, b, *, alpha, tl, tf):
    bsz, seq, d = x.shape
    dff = w1.shape[1]
    return pl.pallas_call(
        functools.partial(_ffn_kernel, alpha=alpha),
        grid=(bsz, seq // tl, dff // tf),
        in_specs=[
            pl.BlockSpec((None, tl, d), lambda bb, t, f: (bb, t, 0)),
            pl.BlockSpec((d, tf), lambda bb, t, f: (0, f)),
            pl.BlockSpec((tf, d), lambda bb, t, f: (f, 0)),
            pl.BlockSpec((1, d), lambda bb, t, f: (0, 0)),
            pl.BlockSpec((1, d), lambda bb, t, f: (0, 0)),
        ],
        out_specs=pl.BlockSpec((None, tl, d), lambda bb, t, f: (bb, t, 0)),
        out_shape=jax.ShapeDtypeStruct((bsz, seq, d), F32),
        scratch_shapes=[pltpu.VMEM((tl, d), F32)],
        compiler_params=pltpu.CompilerParams(
            dimension_semantics=("parallel", "parallel", "arbitrary"),
            vmem_limit_bytes=VMEM_LIMIT),
        name="ffn_ln",
    )(x, w1, w2, g.reshape(1, d), b.reshape(1, d))


def _rope_tables(seq, q_scale):
    pos = jnp.arange(seq, dtype=F32)
    inv_freq = ROPE_THETA ** (-jnp.arange(0, HEAD_DIM, 2, dtype=F32) / HEAD_DIM)
    ang = pos[:, None] * inv_freq[None, :]
    reps = LANES // (HEAD_DIM // 2)
    cos = jnp.tile(jnp.cos(ang), (1, reps))
    sin = jnp.tile(jnp.sin(ang), (1, reps))
    return cos, sin, (cos * q_scale).T, (sin * q_scale).T


def _rotate_half_columns(w):
    d, n = w.shape
    w4 = w.reshape(d, n // HEAD_DIM, 2, HEAD_DIM // 2)
    return jnp.stack([-w4[:, :, 1], w4[:, :, 0]], axis=2).reshape(d, n)


def _ssm_params(lam_re, lam_im, log_dt, b_re, b_im, c_re, c_im, n_half=2):
    g, p = lam_re.shape
    c = b_re.shape[-1]
    lr = lam_re.astype(F32)
    li = lam_im.astype(F32)
    dt = jnp.exp(log_dt.astype(F32))[:, None]
    mag = jnp.exp(lr * dt)
    ar = mag * jnp.cos(li * dt)
    ai = mag * jnp.sin(li * dt)
    den = lr * lr + li * li
    fr = ((ar - 1.0) * lr + ai * li) / den
    fi = (ai * lr - (ar - 1.0) * li) / den
    br = b_re.astype(F32)
    bi = b_im.astype(F32)
    bbar_re = fr[..., None] * br - fi[..., None] * bi
    bbar_im = fr[..., None] * bi + fi[..., None] * br
    gh = g // n_half
    eye = jnp.eye(gh, dtype=F32)

    def blockdiag_in(m):
        m = m.reshape(n_half, gh, p, c)
        return jnp.einsum('hgpc,gk->hgckp', m, eye).reshape(n_half, gh * c, gh * p)

    def blockdiag_out(m):
        m = m.reshape(n_half, gh, c, p)
        return jnp.einsum('hgcp,gk->hgpkc', m, eye).reshape(n_half, gh * p, gh * c)

    bmat = jnp.concatenate([blockdiag_in(bbar_re), blockdiag_in(bbar_im)], axis=2)
    cmat = jnp.concatenate([blockdiag_out(c_re.astype(F32)), blockdiag_out(-c_im.astype(F32))], axis=1)
    a_re = jnp.broadcast_to(ar.reshape(1, g * p), (SUBLANES, g * p))
    a_im = jnp.broadcast_to(ai.reshape(1, g * p), (SUBLANES, g * p))
    return bmat.astype(BF16), cmat.astype(BF16), a_re, a_im


def kernel(x, w_in, w_out, lam_qk, subln_g, ssm_lam_re, ssm_lam_im, ssm_log_dt, ssm_b_re, ssm_b_im,
           ssm_c_re, ssm_c_im, ssm_d, glu_w, glu_b, ln1_g, ln1_b, w_ff1, w_ff2, ln2_g, ln2_b):
    bsz, seq, d_model = x.shape
    depth = w_in.shape[0]
    ssm_w = glu_w.shape[1]
    attn_w = w_out.shape[1] - ssm_w
    n_heads = attn_w // V_DIM
    qk_w = n_heads * 2 * HEAD_DIM
    assert bsz == SUBLANES, "the S5 scan keeps the batch on the sublane axis"
    assert w_in.shape[2] == 2 * qk_w + attn_w + ssm_w
    alpha = (2.0 * depth) ** 0.25
    q_scale = math.log2(math.e) / math.sqrt(HEAD_DIM)

    tl = min(512, seq)
    tq = tl
    tc = min(64, seq)
    tf = 1024

    cos, sin, cost, sint = _rope_tables(seq, q_scale)
    for l in range(depth):
        lam_init = 0.8 - 0.6 * math.exp(-0.3 * l)
        w = w_in[l].astype(F32)
        wq = w[:, :qk_w]
        wk = w[:, qk_w:2 * qk_w]
        wv = w[:, 2 * qk_w:2 * qk_w + attn_w]
        wu = w[:, 2 * qk_w + attn_w:]
        qt, k, vt, u = _in_proj(
            x, wq.T.astype(BF16), _rotate_half_columns(wq).T.astype(BF16), wk.astype(BF16),
            _rotate_half_columns(wk).astype(BF16), wv.T.astype(BF16), wu.astype(BF16),
            cos, sin, cost, sint, tl=tl)

        attn = _attention(qt, k, vt, lam_qk[l].astype(F32), subln_g[l].astype(F32),
                          bsz=bsz, n_heads=n_heads, lam_init=lam_init, tq=tq)

        bmat, cmat, a_re, a_im = _ssm_params(ssm_lam_re[l], ssm_lam_im[l], ssm_log_dt[l],
                                             ssm_b_re[l], ssm_b_im[l], ssm_c_re[l], ssm_c_im[l])
        ssm = _ssm(u.reshape(seq * bsz, ssm_w), bmat, cmat, a_re, a_im,
                   ssm_d[l].astype(F32).reshape(1, ssm_w), glu_w[l].astype(BF16),
                   glu_b[l].astype(F32).reshape(1, ssm_w), tc=tc)
        ssm = ssm.reshape(seq, bsz * ssm_w)

        wo = w_out[l].astype(BF16)
        x = _out_proj(x, attn, ssm, wo[:attn_w], wo[attn_w:], ln1_g[l].astype(F32),
                      ln1_b[l].astype(F32), alpha=alpha, tl=tl)
        x = _ffn(x, w_ff1[l].astype(BF16), w_ff2[l].astype(BF16), ln2_g[l].astype(F32),
                 ln2_b[l].astype(F32), alpha=alpha, tl=min(2 * tl, seq), tf=tf)
    return x
```

```python
import functools
import math

import jax
import jax.numpy as jnp
from jax import lax
from jax.experimental import pallas as pl
from jax.experimental.pallas import tpu as pltpu

F32 = jnp.float32
BF16 = jnp.bfloat16

HEAD_DIM = 64
V_DIM = 2 * HEAD_DIM
SSM_GROUP = 16
SSM_STATE = 64
ROPE_THETA = 10000.0
LN_EPS = 1e-5
RMS_EPS = 1e-5
LANES = 128
SUBLANES = 8
NEG_BIG = -1e30
CHUNK = 128
VMEM_LIMIT = 56 * 1024 * 1024


def _dot(a, b):
    return jnp.dot(a, b, preferred_element_type=F32)


def _layer_norm(y, g, b):
    mu = jnp.mean(y, axis=-1, keepdims=True)
    d = y - mu
    var = jnp.mean(d * d, axis=-1, keepdims=True)
    return d * lax.rsqrt(var + LN_EPS) * g + b


_NT = (((1,), (1,)), ((), ()))


def _in_proj_kernel(x_ref, wqt_ref, wqrt_ref, wk_ref, wkr_ref, wvt_ref, wu_ref,
                    cos_ref, sin_ref, cost_ref, sint_ref, qt_ref, k_ref, vt_ref, u_ref):
    x = x_ref[...].astype(BF16)
    hq = lax.dot_general(wqt_ref[...], x, _NT, preferred_element_type=F32)
    hr = lax.dot_general(wqrt_ref[...], x, _NT, preferred_element_type=F32)
    cost = cost_ref[...]
    sint = sint_ref[...]
    for j in range(hq.shape[0] // LANES):
        sl = slice(j * LANES, (j + 1) * LANES)
        qt_ref[sl, :] = (hq[sl] * cost + hr[sl] * sint).astype(BF16)
    vt_ref[...] = lax.dot_general(wvt_ref[...], x, _NT, preferred_element_type=F32).astype(BF16)
    cos = cos_ref[...]
    sin = sin_ref[...]
    hk = _dot(x, wk_ref[...])
    hkr = _dot(x, wkr_ref[...])
    for j in range(hk.shape[1] // LANES):
        sl = slice(j * LANES, (j + 1) * LANES)
        k_ref[j] = (hk[:, sl] * cos + hkr[:, sl] * sin).astype(BF16)
    u_ref[...] = _dot(x, wu_ref[...])


def _in_proj(x, wqt, wqrt, wk, wkr, wvt, wu, cos, sin, cost, sint, *, tl):
    bsz, seq, d = x.shape
    qw = wqt.shape[0]
    kw = wk.shape[1]
    vw = wvt.shape[0]
    uw = wu.shape[1]
    const = lambda b, t: (0, 0)
    return pl.pallas_call(
        _in_proj_kernel,
        grid=(bsz, seq // tl),
        in_specs=[
            pl.BlockSpec((None, tl, d), lambda b, t: (b, t, 0)),
            pl.BlockSpec(wqt.shape, const),
            pl.BlockSpec(wqrt.shape, const),
            pl.BlockSpec(wk.shape, const),
            pl.BlockSpec(wkr.shape, const),
            pl.BlockSpec(wvt.shape, const),
            pl.BlockSpec(wu.shape, const),
            pl.BlockSpec((tl, LANES), lambda b, t: (t, 0)),
            pl.BlockSpec((tl, LANES), lambda b, t: (t, 0)),
            pl.BlockSpec((LANES, tl), lambda b, t: (0, t)),
            pl.BlockSpec((LANES, tl), lambda b, t: (0, t)),
        ],
        out_specs=[
            pl.BlockSpec((None, qw, tl), lambda b, t: (t, b, 0)),
            pl.BlockSpec((kw // LANES, tl, LANES), lambda b, t: (b, t, 0)),
            pl.BlockSpec((None, vw, tl), lambda b, t: (t, b, 0)),
            pl.BlockSpec((tl, uw), lambda b, t: (t, b)),
        ],
        out_shape=[
            jax.ShapeDtypeStruct((seq // tl, bsz * qw, tl), BF16),
            jax.ShapeDtypeStruct((bsz * kw // LANES, seq, LANES), BF16),
            jax.ShapeDtypeStruct((seq // tl, bsz * vw, tl), BF16),
            jax.ShapeDtypeStruct((seq, bsz * uw), F32),
        ],
        compiler_params=pltpu.CompilerParams(
            dimension_semantics=("parallel", "parallel"), vmem_limit_bytes=VMEM_LIMIT),
        name="in_proj",
    )(x, wqt, wqrt, wk, wkr, wvt, wu, cos, sin, cost, sint)


def _attn_kernel(lq_ref, g_ref, qt_ref, k_ref, vt_ref, o_ref, m_ref, l_ref, acc_ref,
                 s_ref, cm_ref, p_ref, a_ref, *, tq, lam_init):
    qi = pl.program_id(2)
    qt = qt_ref[...].astype(F32)
    rowid = lax.broadcasted_iota(jnp.int32, qt.shape, 0)
    qm = (jnp.where(rowid < HEAD_DIM, qt, 0.0).astype(BF16),
          jnp.where(rowid >= HEAD_DIM, qt, 0.0).astype(BF16))
    m_ref[...] = jnp.full(m_ref.shape, NEG_BIG, F32)
    l_ref[...] = jnp.zeros(l_ref.shape, F32)
    acc_ref[...] = jnp.zeros(acc_ref.shape, F32)
    p_ref[...] = jnp.zeros(p_ref.shape, BF16)
    a_ref[...] = jnp.ones(a_ref.shape, F32)
    n_rg = tq // SUBLANES

    def scores(mp, kt):
        k0 = pl.multiple_of(kt * tq, tq)
        s = _dot(k_ref[pl.ds(k0, tq), :], qm[mp])
        s_ref[mp] = s
        cm_ref[mp] = jnp.max(s.reshape(n_rg, SUBLANES, tq), axis=0)

    def softmax_tile(mp, masked):
        if masked:
            cm = jnp.full((SUBLANES, tq), NEG_BIG, F32)
            for c in range(tq // CHUNK):
                sc = s_ref[mp, c * CHUNK:(c + 1) * CHUNK, :]
                row = lax.broadcasted_iota(jnp.int32, sc.shape, 0) + c * CHUNK
                col = lax.broadcasted_iota(jnp.int32, sc.shape, 1)
                sc = jnp.where(row <= col, sc, NEG_BIG)
                s_ref[mp, c * CHUNK:(c + 1) * CHUNK, :] = sc
                cm = jnp.maximum(cm, jnp.max(sc.reshape(CHUNK // SUBLANES, SUBLANES, tq), axis=0))
        else:
            cm = cm_ref[mp]
        m_prev = m_ref[mp]
        m_new = jnp.maximum(m_prev, jnp.max(cm, axis=0, keepdims=True))
        alpha = jnp.exp2(m_prev - m_new)
        m_ref[mp] = m_new
        lsum = jnp.zeros((SUBLANES, tq), F32)
        for c in range(tq // CHUNK):
            pc = jnp.exp2(s_ref[mp, c * CHUNK:(c + 1) * CHUNK, :] - m_new)
            lsum = lsum + jnp.sum(pc.reshape(CHUNK // SUBLANES, SUBLANES, tq), axis=0)
            p_ref[mp, c * CHUNK:(c + 1) * CHUNK, :] = pc.astype(BF16)
        l_ref[mp] = alpha * l_ref[mp] + lsum
        return alpha

    def pv(kt):
        vt = vt_ref[kt]
        for mp in range(2):
            acc_ref[mp] = a_ref[mp] * acc_ref[mp] + _dot(vt, p_ref[mp])

    def step(kt, masked, prefetch):
        pv(jnp.maximum(kt - 1, 0))
        for mp in range(2):
            a_ref[mp] = softmax_tile(mp, masked)
        if prefetch:
            for mp in range(2):
                scores(mp, kt + 1)

    def body(i, carry):
        step(2 * i, False, True)
        step(2 * i + 1, False, True)
        return carry

    for mp in range(2):
        scores(mp, 0)
    lax.fori_loop(0, qi // 2, body, 0)

    @pl.when(qi % 2 == 1)
    def _():
        step(qi - 1, False, True)

    step(qi, True, False)
    pv(qi)

    lq = lq_ref[...]
    lam = (jnp.exp(jnp.sum(lq[0:1] * lq[1:2], axis=1, keepdims=True))
           - jnp.exp(jnp.sum(lq[2:3] * lq[3:4], axis=1, keepdims=True)) + lam_init)
    l0 = jnp.sum(l_ref[0], axis=0, keepdims=True)
    l1 = jnp.sum(l_ref[1], axis=0, keepdims=True)
    ot = acc_ref[0] / l0 - lam * (acc_ref[1] / l1)
    ms = jnp.mean(ot * ot, axis=0, keepdims=True)
    ot = ot * lax.rsqrt(ms + RMS_EPS) * g_ref[...] * (1.0 - lam_init)
    o_ref[...] = ot.T.astype(o_ref.dtype)


def _attention(qt, k, vt, lam_qk, subln_g, *, bsz, n_heads, lam_init, tq):
    seq = k.shape[1]
    return pl.pallas_call(
        functools.partial(_attn_kernel, tq=tq, lam_init=lam_init),
        grid=(bsz, n_heads, seq // tq),
        in_specs=[
            pl.BlockSpec(lam_qk.shape, lambda b, h, i: (0, 0)),
            pl.BlockSpec((V_DIM, 1), lambda b, h, i: (0, 0)),
            pl.BlockSpec((None, V_DIM, tq), lambda b, h, i: (i, b * n_heads + h, 0)),
            pl.BlockSpec((None, seq, V_DIM), lambda b, h, i: (b * n_heads + h, 0, 0)),
            pl.BlockSpec((seq // tq, V_DIM, tq), lambda b, h, i: (0, b * n_heads + h, 0)),
        ],
        out_specs=pl.BlockSpec((None, tq, V_DIM), lambda b, h, i: (b * n_heads + h, i, 0)),
        out_shape=jax.ShapeDtypeStruct((bsz * n_heads, seq, V_DIM), BF16),
        scratch_shapes=[
            pltpu.VMEM((2, 1, tq), F32),
            pltpu.VMEM((2, SUBLANES, tq), F32),
            pltpu.VMEM((2, V_DIM, tq), F32),
            pltpu.VMEM((2, tq, tq), F32),
            pltpu.VMEM((2, SUBLANES, tq), F32),
            pltpu.VMEM((2, tq, tq), BF16),
            pltpu.VMEM((2, 1, tq), F32),
        ],
        compiler_params=pltpu.CompilerParams(
            dimension_semantics=("parallel", "parallel", "arbitrary"),
            vmem_limit_bytes=VMEM_LIMIT),
        name="diff_attn",
    )(lam_qk, subln_g.reshape(V_DIM, 1), qt, k, vt)


def _gelu_tanh(y):
    c = math.sqrt(2.0 / math.pi)
    return 0.5 * y * (1.0 + jnp.tanh(c * (y + 0.044715 * (y * y * y))))


def _ssm_kernel(u_ref, bmat_ref, cmat_ref, ar_ref, ai_ref, d_ref, gw_ref, gb_ref,
                out_ref, bu_ref, st_ref, *, tc, slab):
    n_half = bmat_ref.shape[0]
    uk = bmat_ref.shape[1]
    hw = bmat_ref.shape[2]
    cw = hw // 2

    @pl.when(pl.program_id(0) == 0)
    def _():
        st_ref[...] = jnp.zeros(st_ref.shape, F32)

    u = pltpu.einshape("t(bc)->(tb)c", u_ref[...], b=SUBLANES)
    ub = u.astype(BF16)
    for hf in range(n_half):
        bu_ref[:, hf * hw:(hf + 1) * hw] = _dot(ub[:, hf * uk:(hf + 1) * uk], bmat_ref[hf])

    for hf in range(n_half):
        for sl in range(cw // slab):
            cr = hf * hw + sl * slab
            ci = cr + cw
            ca = hf * cw + sl * slab
            a_re = ar_ref[:, ca:ca + slab]
            a_im = ai_ref[:, ca:ca + slab]

            def body(t, carry, cr=cr, ci=ci, a_re=a_re, a_im=a_im):
                s_re, s_im = carry
                r0 = pl.multiple_of(t * SUBLANES, SUBLANES)
                n_re = a_re * s_re - a_im * s_im + bu_ref[pl.ds(r0, SUBLANES), cr:cr + slab]
                n_im = a_re * s_im + a_im * s_re + bu_ref[pl.ds(r0, SUBLANES), ci:ci + slab]
                bu_ref[pl.ds(r0, SUBLANES), cr:cr + slab] = n_re
                bu_ref[pl.ds(r0, SUBLANES), ci:ci + slab] = n_im
                return n_re, n_im

            s_re, s_im = lax.fori_loop(
                0, tc, body, (st_ref[:, cr:cr + slab], st_ref[:, ci:ci + slab]), unroll=8)
            st_ref[:, cr:cr + slab] = s_re
            st_ref[:, ci:ci + slab] = s_im

    ys = []
    for hf in range(n_half):
        s = bu_ref[:, hf * hw:(hf + 1) * hw].astype(BF16)
        ys.append(_dot(s, cmat_ref[hf]))
    y = jnp.concatenate(ys, axis=1) + d_ref[...] * u
    y = _gelu_tanh(y)
    z = _dot(y.astype(BF16), gw_ref[...]) + gb_ref[...]
    out = pltpu.einshape("(tb)c->t(bc)", y * jax.nn.sigmoid(z), b=SUBLANES)
    out_ref[...] = out.astype(out_ref.dtype)


def _ssm(u, bmat, cmat, a_re, a_im, d_skip, glu_w, glu_b, *, tc, slab=512):
    seq, bw = u.shape
    blk = tc * SUBLANES
    state_w = bmat.shape[0] * bmat.shape[2]
    c2 = lambda t: (0, 0)
    c3 = lambda t: (0, 0, 0)
    return pl.pallas_call(
        functools.partial(_ssm_kernel, tc=tc, slab=slab),
        grid=(seq // tc,),
        in_specs=[
            pl.BlockSpec((tc, bw), lambda t: (t, 0)),
            pl.BlockSpec(bmat.shape, c3),
            pl.BlockSpec(cmat.shape, c3),
            pl.BlockSpec(a_re.shape, c2),
            pl.BlockSpec(a_im.shape, c2),
            pl.BlockSpec(d_skip.shape, c2),
            pl.BlockSpec(glu_w.shape, c2),
            pl.BlockSpec(glu_b.shape, c2),
        ],
        out_specs=pl.BlockSpec((tc, bw), lambda t: (t, 0)),
        out_shape=jax.ShapeDtypeStruct((seq, bw), BF16),
        scratch_shapes=[
            pltpu.VMEM((blk, state_w), F32),
            pltpu.VMEM((SUBLANES, state_w), F32),
        ],
        compiler_params=pltpu.CompilerParams(
            dimension_semantics=("arbitrary",), vmem_limit_bytes=VMEM_LIMIT),
        name="s5_ssm",
    )(u, bmat, cmat, a_re, a_im, d_skip, glu_w, glu_b)


def _out_proj_kernel(x_ref, a_ref, s_ref, wa_ref, ws_ref, g_ref, b_ref, o_ref, *, alpha):
    a = jnp.concatenate([a_ref[h] for h in range(a_ref.shape[0])], axis=1)
    mix = _dot(a, wa_ref[...]) + _dot(s_ref[...], ws_ref[...])
    y = alpha * x_ref[...] + mix
    o_ref[...] = _layer_norm(y, g_ref[...], b_ref[...])


def _out_proj(x, attn, ssm, wa, ws, g, b, *, alpha, tl):
    bsz, seq, d = x.shape
    aw = wa.shape[0]
    sw = ws.shape[0]
    const = lambda bb, t: (0, 0)
    return pl.pallas_call(
        functools.partial(_out_proj_kernel, alpha=alpha),
        grid=(bsz, seq // tl),
        in_specs=[
            pl.BlockSpec((None, tl, d), lambda bb, t: (bb, t, 0)),
            pl.BlockSpec((aw // V_DIM, tl, V_DIM), lambda bb, t: (bb, t, 0)),
            pl.BlockSpec((tl, sw), lambda bb, t: (t, bb)),
            pl.BlockSpec(wa.shape, const),
            pl.BlockSpec(ws.shape, const),
            pl.BlockSpec((1, d), const),
            pl.BlockSpec((1, d), const),
        ],
        out_specs=pl.BlockSpec((None, tl, d), lambda bb, t: (bb, t, 0)),
        out_shape=jax.ShapeDtypeStruct((bsz, seq, d), F32),
        compiler_params=pltpu.CompilerParams(
            dimension_semantics=("parallel", "parallel"), vmem_limit_bytes=VMEM_LIMIT),
        name="out_proj_ln",
    )(x, attn, ssm, wa, ws, g.reshape(1, d), b.reshape(1, d))


def _ffn_kernel(x_ref, w1_ref, w2_ref, g_ref, b_ref, o_ref, acc_ref, *, alpha):
    f = pl.program_id(2)

    @pl.when(f == 0)
    def _():
        acc_ref[...] = jnp.zeros(acc_ref.shape, F32)

    h = jnp.maximum(_dot(x_ref[...].astype(BF16), w1_ref[...]), 0.0)
    acc_ref[...] += _dot((h * h).astype(BF16), w2_ref[...])

    @pl.when(f == pl.num_programs(2) - 1)
    def _():
        y = alpha * x_ref[...] + acc_ref[...]
        o_ref[...] = _layer_norm(y, g_ref[...], b_ref[...])


def _ffn(x, w1, w2, g, b, *, alpha, tl, tf):
    bsz, seq, d = x.shape
    dff = w1.shape[1]
    return pl.pallas_call(
        functools.partial(_ffn_kernel, alpha=alpha),
        grid=(bsz, seq // tl, dff // tf),
        in_specs=[
            pl.BlockSpec((None, tl, d), lambda bb, t, f: (bb, t, 0)),
            pl.BlockSpec((d, tf), lambda bb, t, f: (0, f)),
            pl.BlockSpec((tf, d), lambda bb, t, f: (f, 0)),
            pl.BlockSpec((1, d), lambda bb, t, f: (0, 0)),
            pl.BlockSpec((1, d), lambda bb, t, f: (0, 0)),
        ],
        out_specs=pl.BlockSpec((None, tl, d), lambda bb, t, f: (bb, t, 0)),
        out_shape=jax.ShapeDtypeStruct((bsz, seq, d), F32),
        scratch_shapes=[pltpu.VMEM((tl, d), F32)],
        compiler_params=pltpu.CompilerParams(
            dimension_semantics=("parallel", "parallel", "arbitrary"),
            vmem_limit_bytes=VMEM_LIMIT),
        name="ffn_ln",
    )(x, w1, w2, g.reshape(1, d), b.reshape(1, d))


def _rope_tables(seq, q_scale):
    pos = jnp.arange(seq, dtype=F32)
    inv_freq = ROPE_THETA ** (-jnp.arange(0, HEAD_DIM, 2, dtype=F32) / HEAD_DIM)
    ang = pos[:, None] * inv_freq[None, :]
    reps = LANES // (HEAD_DIM // 2)
    cos = jnp.tile(jnp.cos(ang), (1, reps))
    sin = jnp.tile(jnp.sin(ang), (1, reps))
    return cos, sin, (cos * q_scale).T, (sin * q_scale).T


def _rotate_half_columns(w):
    d, n = w.shape
    w4 = w.reshape(d, n // HEAD_DIM, 2, HEAD_DIM // 2)
    return jnp.stack([-w4[:, :, 1], w4[:, :, 0]], axis=2).reshape(d, n)


def _ssm_params(lam_re, lam_im, log_dt, b_re, b_im, c_re, c_im, n_half=2):
    g, p = lam_re.shape
    c = b_re.shape[-1]
    lr = lam_re.astype(F32)
    li = lam_im.astype(F32)
    dt = jnp.exp(log_dt.astype(F32))[:, None]
    mag = jnp.exp(lr * dt)
    ar = mag * jnp.cos(li * dt)
    ai = mag * jnp.sin(li * dt)
    den = lr * lr + li * li
    fr = ((ar - 1.0) * lr + ai * li) / den
    fi = (ai * lr - (ar - 1.0) * li) / den
    br = b_re.astype(F32)
    bi = b_im.astype(F32)
    bbar_re = fr[..., None] * br - fi[..., None] * bi
    bbar_im = fr[..., None] * bi + fi[..., None] * br
    gh = g // n_half
    eye = jnp.eye(gh, dtype=F32)

    def blockdiag_in(m):
        m = m.reshape(n_half, gh, p, c)
        return jnp.einsum('hgpc,gk->hgckp', m, eye).reshape(n_half, gh * c, gh * p)

    def blockdiag_out(m):
        m = m.reshape(n_half, gh, c, p)
        return jnp.einsum('hgcp,gk->hgpkc', m, eye).reshape(n_half, gh * p, gh * c)

    bmat = jnp.concatenate([blockdiag_in(bbar_re), blockdiag_in(bbar_im)], axis=2)
    cmat = jnp.concatenate([blockdiag_out(c_re.astype(F32)), blockdiag_out(-c_im.astype(F32))], axis=1)
    a_re = jnp.broadcast_to(ar.reshape(1, g * p), (SUBLANES, g * p))
    a_im = jnp.broadcast_to(ai.reshape(1, g * p), (SUBLANES, g * p))
    return bmat.astype(BF16), cmat.astype(BF16), a_re, a_im


def kernel(x, w_in, w_out, lam_qk, subln_g, ssm_lam_re, ssm_lam_im, ssm_log_dt, ssm_b_re, ssm_b_im,
           ssm_c_re, ssm_c_im, ssm_d, glu_w, glu_b, ln1_g, ln1_b, w_ff1, w_ff2, ln2_g, ln2_b):
    bsz, seq, d_model = x.shape
    depth = w_in.shape[0]
    ssm_w = glu_w.shape[1]
    attn_w = w_out.shape[1] - ssm_w
    n_heads = attn_w // V_DIM
    qk_w = n_heads * 2 * HEAD_DIM
    assert bsz == SUBLANES, "the S5 scan keeps the batch on the sublane axis"
    assert w_in.shape[2] == 2 * qk_w + attn_w + ssm_w
    alpha = (2.0 * depth) ** 0.25
    q_scale = math.log2(math.e) / math.sqrt(HEAD_DIM)

    tl = min(512, seq)
    tq = tl
    tc = min(64, seq)
    tf = 1024

    cos, sin, cost, sint = _rope_tables(seq, q_scale)
    for l in range(depth):
        lam_init = 0.8 - 0.6 * math.exp(-0.3 * l)
        w = w_in[l].astype(F32)
        wq = w[:, :qk_w]
        wk = w[:, qk_w:2 * qk_w]
        wv = w[:, 2 * qk_w:2 * qk_w + attn_w]
        wu = w[:, 2 * qk_w + attn_w:]
        qt, k, vt, u = _in_proj(
            x, wq.T.astype(BF16), _rotate_half_columns(wq).T.astype(BF16), wk.astype(BF16),
            _rotate_half_columns(wk).astype(BF16), wv.T.astype(BF16), wu.astype(BF16),
            cos, sin, cost, sint, tl=tl)

        attn = _attention(qt, k, vt, lam_qk[l].astype(F32), subln_g[l].astype(F32),
                          bsz=bsz, n_heads=n_heads, lam_init=lam_init, tq=tq)

        bmat, cmat, a_re, a_im = _ssm_params(ssm_lam_re[l], ssm_lam_im[l], ssm_log_dt[l],
                                             ssm_b_re[l], ssm_b_im[l], ssm_c_re[l], ssm_c_im[l])
        ssm = _ssm(u, bmat, cmat, a_re, a_im,
                   ssm_d[l].astype(F32).reshape(1, ssm_w), glu_w[l].astype(BF16),
                   glu_b[l].astype(F32).reshape(1, ssm_w), tc=tc)

        wo = w_out[l].astype(BF16)
        x = _out_proj(x, attn, ssm, wo[:attn_w], wo[attn_w:], ln1_g[l].astype(F32),
                      ln1_b[l].astype(F32), alpha=alpha, tl=tl)
        x = _ffn(x, w_ff1[l].astype(BF16), w_ff2[l].astype(BF16), ln2_g[l].astype(F32),
                 ln2_b[l].astype(F32), alpha=alpha, tl=min(2 * tl, seq), tf=tf)
    return x
```

```python
import functools
import math

import jax
import jax.numpy as jnp
from jax import lax
from jax.experimental import pallas as pl
from jax.experimental.pallas import tpu as pltpu

F32 = jnp.float32
BF16 = jnp.bfloat16

HEAD_DIM = 64
V_DIM = 2 * HEAD_DIM
SSM_GROUP = 16
SSM_STATE = 64
ROPE_THETA = 10000.0
LN_EPS = 1e-5
RMS_EPS = 1e-5
LANES = 128
SUBLANES = 8
NEG_BIG = -1e30
CHUNK = 128
VMEM_LIMIT = 56 * 1024 * 1024


def _dot(a, b):
    return jnp.dot(a, b, preferred_element_type=F32)


def _layer_norm(y, g, b):
    mu = jnp.mean(y, axis=-1, keepdims=True)
    d = y - mu
    var = jnp.mean(d * d, axis=-1, keepdims=True)
    return d * lax.rsqrt(var + LN_EPS) * g + b


_NT = (((1,), (1,)), ((), ()))


def _in_proj_kernel(x_ref, wqt_ref, wqrt_ref, wk_ref, wkr_ref, wvt_ref, wu_ref,
                    cos_ref, sin_ref, cost_ref, sint_ref, qt_ref, k_ref, vt_ref, u_ref):
    x = x_ref[...].astype(BF16)
    hq = lax.dot_general(wqt_ref[...], x, _NT, preferred_element_type=F32)
    hr = lax.dot_general(wqrt_ref[...], x, _NT, preferred_element_type=F32)
    cost = cost_ref[...]
    sint = sint_ref[...]
    for j in range(hq.shape[0] // LANES):
        sl = slice(j * LANES, (j + 1) * LANES)
        qt_ref[sl, :] = (hq[sl] * cost + hr[sl] * sint).astype(BF16)
    vt_ref[...] = lax.dot_general(wvt_ref[...], x, _NT, preferred_element_type=F32).astype(BF16)
    cos = cos_ref[...]
    sin = sin_ref[...]
    hk = _dot(x, wk_ref[...])
    hkr = _dot(x, wkr_ref[...])
    for j in range(hk.shape[1] // LANES):
        sl = slice(j * LANES, (j + 1) * LANES)
        k_ref[j] = (hk[:, sl] * cos + hkr[:, sl] * sin).astype(BF16)
    u_ref[...] = _dot(x, wu_ref[...])


def _in_proj(x, wqt, wqrt, wk, wkr, wvt, wu, cos, sin, cost, sint, *, tl):
    bsz, seq, d = x.shape
    qw = wqt.shape[0]
    kw = wk.shape[1]
    vw = wvt.shape[0]
    uw = wu.shape[1]
    const = lambda b, t: (0, 0)
    return pl.pallas_call(
        _in_proj_kernel,
        grid=(bsz, seq // tl),
        in_specs=[
            pl.BlockSpec((None, tl, d), lambda b, t: (b, t, 0)),
            pl.BlockSpec(wqt.shape, const),
            pl.BlockSpec(wqrt.shape, const),
            pl.BlockSpec(wk.shape, const),
            pl.BlockSpec(wkr.shape, const),
            pl.BlockSpec(wvt.shape, const),
            pl.BlockSpec(wu.shape, const),
            pl.BlockSpec((tl, LANES), lambda b, t: (t, 0)),
            pl.BlockSpec((tl, LANES), lambda b, t: (t, 0)),
            pl.BlockSpec((LANES, tl), lambda b, t: (0, t)),
            pl.BlockSpec((LANES, tl), lambda b, t: (0, t)),
        ],
        out_specs=[
            pl.BlockSpec((None, qw, tl), lambda b, t: (t, b, 0)),
            pl.BlockSpec((kw // LANES, tl, LANES), lambda b, t: (b, t, 0)),
            pl.BlockSpec((None, vw, tl), lambda b, t: (t, b, 0)),
            pl.BlockSpec((tl, uw), lambda b, t: (t, b)),
        ],
        out_shape=[
            jax.ShapeDtypeStruct((seq // tl, bsz * qw, tl), BF16),
            jax.ShapeDtypeStruct((bsz * kw // LANES, seq, LANES), BF16),
            jax.ShapeDtypeStruct((seq // tl, bsz * vw, tl), BF16),
            jax.ShapeDtypeStruct((seq, bsz * uw), F32),
        ],
        compiler_params=pltpu.CompilerParams(
            dimension_semantics=("parallel", "parallel"), vmem_limit_bytes=VMEM_LIMIT),
        name="in_proj",
    )(x, wqt, wqrt, wk, wkr, wvt, wu, cos, sin, cost, sint)


def _attn_q_tile(qi, lq_ref, g_ref, qt_ref, k_ref, vt_ref, o_ref, m_ref, l_ref, acc_ref,
                 s_ref, cm_ref, p_ref, a_ref, *, tq, lam_init):
    qt = qt_ref[qi].astype(F32)
    rowid = lax.broadcasted_iota(jnp.int32, qt.shape, 0)
    qm = (jnp.where(rowid < HEAD_DIM, qt, 0.0).astype(BF16),
          jnp.where(rowid >= HEAD_DIM, qt, 0.0).astype(BF16))
    m_ref[...] = jnp.full(m_ref.shape, NEG_BIG, F32)
    l_ref[...] = jnp.zeros(l_ref.shape, F32)
    acc_ref[...] = jnp.zeros(acc_ref.shape, F32)
    p_ref[...] = jnp.zeros(p_ref.shape, BF16)
    a_ref[...] = jnp.ones(a_ref.shape, F32)
    n_rg = tq // SUBLANES

    def scores(mp, kt):
        k0 = pl.multiple_of(kt * tq, tq)
        s = _dot(k_ref[pl.ds(k0, tq), :], qm[mp])
        s_ref[mp] = s
        cm_ref[mp] = jnp.max(s.reshape(n_rg, SUBLANES, tq), axis=0)

    def softmax_tile(mp, masked):
        if masked:
            cm = jnp.full((SUBLANES, tq), NEG_BIG, F32)
            for c in range(tq // CHUNK):
                sc = s_ref[mp, c * CHUNK:(c + 1) * CHUNK, :]
                row = lax.broadcasted_iota(jnp.int32, sc.shape, 0) + c * CHUNK
                col = lax.broadcasted_iota(jnp.int32, sc.shape, 1)
                sc = jnp.where(row <= col, sc, NEG_BIG)
                s_ref[mp, c * CHUNK:(c + 1) * CHUNK, :] = sc
                cm = jnp.maximum(cm, jnp.max(sc.reshape(CHUNK // SUBLANES, SUBLANES, tq), axis=0))
        else:
            cm = cm_ref[mp]
        m_prev = m_ref[mp]
        m_new = jnp.maximum(m_prev, jnp.max(cm, axis=0, keepdims=True))
        alpha = jnp.exp2(m_prev - m_new)
        m_ref[mp] = m_new
        lsum = jnp.zeros((SUBLANES, tq), F32)
        for c in range(tq // CHUNK):
            pc = jnp.exp2(s_ref[mp, c * CHUNK:(c + 1) * CHUNK, :] - m_new)
            lsum = lsum + jnp.sum(pc.reshape(CHUNK // SUBLANES, SUBLANES, tq), axis=0)
            p_ref[mp, c * CHUNK:(c + 1) * CHUNK, :] = pc.astype(BF16)
        l_ref[mp] = alpha * l_ref[mp] + lsum
        return alpha

    def pv(kt):
        vt = vt_ref[kt]
        for mp in range(2):
            acc_ref[mp] = a_ref[mp] * acc_ref[mp] + _dot(vt, p_ref[mp])

    def step(kt, masked, prefetch):
        pv(jnp.maximum(kt - 1, 0))
        for mp in range(2):
            a_ref[mp] = softmax_tile(mp, masked)
        if prefetch:
            for mp in range(2):
                scores(mp, kt + 1)

    def body(i, carry):
        step(2 * i, False, True)
        step(2 * i + 1, False, True)
        return carry

    for mp in range(2):
        scores(mp, 0)
    lax.fori_loop(0, qi // 2, body, 0)

    @pl.when(qi % 2 == 1)
    def _():
        step(qi - 1, False, True)

    step(qi, True, False)
    pv(qi)

    lq = lq_ref[...]
    lam = (jnp.exp(jnp.sum(lq[0:1] * lq[1:2], axis=1, keepdims=True))
           - jnp.exp(jnp.sum(lq[2:3] * lq[3:4], axis=1, keepdims=True)) + lam_init)
    l0 = jnp.sum(l_ref[0], axis=0, keepdims=True)
    l1 = jnp.sum(l_ref[1], axis=0, keepdims=True)
    ot = acc_ref[0] / l0 - lam * (acc_ref[1] / l1)
    ms = jnp.mean(ot * ot, axis=0, keepdims=True)
    ot = ot * lax.rsqrt(ms + RMS_EPS) * g_ref[...] * (1.0 - lam_init)
    o_ref[pl.ds(pl.multiple_of(qi * tq, tq), tq), :] = ot.T.astype(o_ref.dtype)


def _attn_kernel(*refs, tq, lam_init):
    n_q = refs[2].shape[0]

    def q_tile(qi, carry):
        _attn_q_tile(qi, *refs, tq=tq, lam_init=lam_init)
        return carry

    lax.fori_loop(0, n_q, q_tile, 0)


def _attention(qt, k, vt, lam_qk, subln_g, *, bsz, n_heads, lam_init, tq):
    seq = k.shape[1]
    return pl.pallas_call(
        functools.partial(_attn_kernel, tq=tq, lam_init=lam_init),
        grid=(bsz, n_heads),
        in_specs=[
            pl.BlockSpec(lam_qk.shape, lambda b, h: (0, 0)),
            pl.BlockSpec((V_DIM, 1), lambda b, h: (0, 0)),
            pl.BlockSpec((seq // tq, V_DIM, tq), lambda b, h: (0, b * n_heads + h, 0)),
            pl.BlockSpec((None, seq, V_DIM), lambda b, h: (b * n_heads + h, 0, 0)),
            pl.BlockSpec((seq // tq, V_DIM, tq), lambda b, h: (0, b * n_heads + h, 0)),
        ],
        out_specs=pl.BlockSpec((None, seq, V_DIM), lambda b, h: (b * n_heads + h, 0, 0)),
        out_shape=jax.ShapeDtypeStruct((bsz * n_heads, seq, V_DIM), BF16),
        scratch_shapes=[
            pltpu.VMEM((2, 1, tq), F32),
            pltpu.VMEM((2, SUBLANES, tq), F32),
            pltpu.VMEM((2, V_DIM, tq), F32),
            pltpu.VMEM((2, tq, tq), F32),
            pltpu.VMEM((2, SUBLANES, tq), F32),
            pltpu.VMEM((2, tq, tq), BF16),
            pltpu.VMEM((2, 1, tq), F32),
        ],
        compiler_params=pltpu.CompilerParams(
            dimension_semantics=("parallel", "parallel"), vmem_limit_bytes=VMEM_LIMIT),
        name="diff_attn",
    )(lam_qk, subln_g.reshape(V_DIM, 1), qt, k, vt)


def _gelu_tanh(y):
    c = math.sqrt(2.0 / math.pi)
    return 0.5 * y * (1.0 + jnp.tanh(c * (y + 0.044715 * (y * y * y))))


def _ssm_kernel(u_ref, bmat_ref, cmat_ref, ar_ref, ai_ref, d_ref, gw_ref, gb_ref,
                out_ref, bu_ref, st_ref, *, tc, slab):
    n_half = bmat_ref.shape[0]
    uk = bmat_ref.shape[1]
    hw = bmat_ref.shape[2]
    cw = hw // 2

    @pl.when(pl.program_id(0) == 0)
    def _():
        st_ref[...] = jnp.zeros(st_ref.shape, F32)

    u = pltpu.einshape("t(bc)->(tb)c", u_ref[...], b=SUBLANES)
    ub = u.astype(BF16)
    for hf in range(n_half):
        bu_ref[:, hf * hw:(hf + 1) * hw] = _dot(ub[:, hf * uk:(hf + 1) * uk], bmat_ref[hf])

    for hf in range(n_half):
        for sl in range(cw // slab):
            cr = hf * hw + sl * slab
            ci = cr + cw
            ca = hf * cw + sl * slab
            a_re = ar_ref[:, ca:ca + slab]
            a_im = ai_ref[:, ca:ca + slab]

            def body(t, carry, cr=cr, ci=ci, a_re=a_re, a_im=a_im):
                s_re, s_im = carry
                r0 = pl.multiple_of(t * SUBLANES, SUBLANES)
                n_re = a_re * s_re - a_im * s_im + bu_ref[pl.ds(r0, SUBLANES), cr:cr + slab]
                n_im = a_re * s_im + a_im * s_re + bu_ref[pl.ds(r0, SUBLANES), ci:ci + slab]
                bu_ref[pl.ds(r0, SUBLANES), cr:cr + slab] = n_re
                bu_ref[pl.ds(r0, SUBLANES), ci:ci + slab] = n_im
                return n_re, n_im

            s_re, s_im = lax.fori_loop(
                0, tc, body, (st_ref[:, cr:cr + slab], st_ref[:, ci:ci + slab]), unroll=8)
            st_ref[:, cr:cr + slab] = s_re
            st_ref[:, ci:ci + slab] = s_im

    ys = []
    for hf in range(n_half):
        s = bu_ref[:, hf * hw:(hf + 1) * hw].astype(BF16)
        ys.append(_dot(s, cmat_ref[hf]))
    y = jnp.concatenate(ys, axis=1) + d_ref[...] * u
    y = _gelu_tanh(y)
    z = _dot(y.astype(BF16), gw_ref[...]) + gb_ref[...]
    out = pltpu.einshape("(tb)c->t(bc)", y * jax.nn.sigmoid(z), b=SUBLANES)
    out_ref[...] = out.astype(out_ref.dtype)


def _ssm(u, bmat, cmat, a_re, a_im, d_skip, glu_w, glu_b, *, tc, slab=512):
    seq, bw = u.shape
    blk = tc * SUBLANES
    state_w = bmat.shape[0] * bmat.shape[2]
    c2 = lambda t: (0, 0)
    c3 = lambda t: (0, 0, 0)
    return pl.pallas_call(
        functools.partial(_ssm_kernel, tc=tc, slab=slab),
        grid=(seq // tc,),
        in_specs=[
            pl.BlockSpec((tc, bw), lambda t: (t, 0)),
            pl.BlockSpec(bmat.shape, c3),
            pl.BlockSpec(cmat.shape, c3),
            pl.BlockSpec(a_re.shape, c2),
            pl.BlockSpec(a_im.shape, c2),
            pl.BlockSpec(d_skip.shape, c2),
            pl.BlockSpec(glu_w.shape, c2),
            pl.BlockSpec(glu_b.shape, c2),
        ],
        out_specs=pl.BlockSpec((tc, bw), lambda t: (t, 0)),
        out_shape=jax.ShapeDtypeStruct((seq, bw), BF16),
        scratch_shapes=[
            pltpu.VMEM((blk, state_w), F32),
            pltpu.VMEM((SUBLANES, state_w), F32),
        ],
        compiler_params=pltpu.CompilerParams(
            dimension_semantics=("arbitrary",), vmem_limit_bytes=VMEM_LIMIT),
        name="s5_ssm",
    )(u, bmat, cmat, a_re, a_im, d_skip, glu_w, glu_b)


def _out_proj_kernel(x_ref, a_ref, s_ref, wa_ref, ws_ref, g_ref, b_ref, o_ref, *, alpha):
    a = jnp.concatenate([a_ref[h] for h in range(a_ref.shape[0])], axis=1)
    mix = _dot(a, wa_ref[...]) + _dot(s_ref[...], ws_ref[...])
    y = alpha * x_ref[...] + mix
    o_ref[...] = _layer_norm(y, g_ref[...], b_ref[...])


def _out_proj(x, attn, ssm, wa, ws, g, b, *, alpha, tl):
    bsz, seq, d = x.shape
    aw = wa.shape[0]
    sw = ws.shape[0]
    const = lambda bb, t: (0, 0)
    return pl.pallas_call(
        functools.partial(_out_proj_kernel, alpha=alpha),
        grid=(bsz, seq // tl),
        in_specs=[
            pl.BlockSpec((None, tl, d), lambda bb, t: (bb, t, 0)),
            pl.BlockSpec((aw // V_DIM, tl, V_DIM), lambda bb, t: (bb, t, 0)),
            pl.BlockSpec((tl, sw), lambda bb, t: (t, bb)),
            pl.BlockSpec(wa.shape, const),
            pl.BlockSpec(ws.shape, const),
            pl.BlockSpec((1, d), const),
            pl.BlockSpec((1, d), const),
        ],
        out_specs=pl.BlockSpec((None, tl, d), lambda bb, t: (bb, t, 0)),
        out_shape=jax.ShapeDtypeStruct((bsz, seq, d), F32),
        compiler_params=pltpu.CompilerParams(
            dimension_semantics=("parallel", "parallel"), vmem_limit_bytes=VMEM_LIMIT),
        name="out_proj_ln",
    )(x, attn, ssm, wa, ws, g.reshape(1, d), b.reshape(1, d))


def _ffn_kernel(x_ref, w1_ref, w2_ref, g_ref, b_ref, o_ref, acc_ref, *, alpha):
    f = pl.program_id(2)

    @pl.when(f == 0)
    def _():
        acc_ref[...] = jnp.zeros(acc_ref.shape, F32)

    h = jnp.maximum(_dot(x_ref[...].astype(BF16), w1_ref[...]), 0.0)
    acc_ref[...] += _dot((h * h).astype(BF16), w2_ref[...])

    @pl.when(f == pl.num_programs(2) - 1)
    def _():
        y = alpha * x_ref[...] + acc_ref[...]
        o_ref[...] = _layer_norm(y, g_ref[...], b_ref[...])


def _ffn(x, w1, w2, g, b, *, alpha, tl, tf):
    bsz, seq, d = x.shape
    dff = w1.shape[1]
    return pl.pallas_call(
        functools.partial(_ffn_kernel, alpha=alpha),
        grid=(bsz, seq // tl, dff // tf),
        in_specs=[
            pl.BlockSpec((None, tl, d), lambda bb, t, f: (bb, t, 0)),
            pl.BlockSpec((d, tf), lambda bb, t, f: (0, f)),
            pl.BlockSpec((tf, d), lambda bb, t, f: (f, 0)),
            pl.BlockSpec((1, d), lambda bb, t, f: (0, 0)),
            pl.BlockSpec((1, d), lambda bb, t, f: (0, 0)),
        ],
        out_specs=pl.BlockSpec((None, tl, d), lambda bb, t, f: (bb, t, 0)),
        out_shape=jax.ShapeDtypeStruct((bsz, seq, d), F32),
        scratch_shapes=[pltpu.VMEM((tl, d), F32)],
        compiler_params=pltpu.CompilerParams(
            dimension_semantics=("parallel", "parallel", "arbitrary"),
            vmem_limit_bytes=VMEM_LIMIT),
        name="ffn_ln",
    )(x, w1, w2, g.reshape(1, d), b.reshape(1, d))


def _rope_tables(seq, q_scale):
    pos = jnp.arange(seq, dtype=F32)
    inv_freq = ROPE_THETA ** (-jnp.arange(0, HEAD_DIM, 2, dtype=F32) / HEAD_DIM)
    ang = pos[:, None] * inv_freq[None, :]
    reps = LANES // (HEAD_DIM // 2)
    cos = jnp.tile(jnp.cos(ang), (1, reps))
    sin = jnp.tile(jnp.sin(ang), (1, reps))
    return cos, sin, (cos * q_scale).T, (sin * q_scale).T


def _rotate_half_columns(w):
    d, n = w.shape
    w4 = w.reshape(d, n // HEAD_DIM, 2, HEAD_DIM // 2)
    return jnp.stack([-w4[:, :, 1], w4[:, :, 0]], axis=2).reshape(d, n)


def _ssm_params(lam_re, lam_im, log_dt, b_re, b_im, c_re, c_im, n_half=2):
    g, p = lam_re.shape
    c = b_re.shape[-1]
    lr = lam_re.astype(F32)
    li = lam_im.astype(F32)
    dt = jnp.exp(log_dt.astype(F32))[:, None]
    mag = jnp.exp(lr * dt)
    ar = mag * jnp.cos(li * dt)
    ai = mag * jnp.sin(li * dt)
    den = lr * lr + li * li
    fr = ((ar - 1.0) * lr + ai * li) / den
    fi = (ai * lr - (ar - 1.0) * li) / den
    br = b_re.astype(F32)
    bi = b_im.astype(F32)
    bbar_re = fr[..., None] * br - fi[..., None] * bi
    bbar_im = fr[..., None] * bi + fi[..., None] * br
    gh = g // n_half
    eye = jnp.eye(gh, dtype=F32)

    def blockdiag_in(m):
        m = m.reshape(n_half, gh, p, c)
        return jnp.einsum('hgpc,gk->hgckp', m, eye).reshape(n_half, gh * c, gh * p)

    def blockdiag_out(m):
        m = m.reshape(n_half, gh, c, p)
        return jnp.einsum('hgcp,gk->hgpkc', m, eye).reshape(n_half, gh * p, gh * c)

    bmat = jnp.concatenate([blockdiag_in(bbar_re), blockdiag_in(bbar_im)], axis=2)
    cmat = jnp.concatenate([blockdiag_out(c_re.astype(F32)), blockdiag_out(-c_im.astype(F32))], axis=1)
    a_re = jnp.broadcast_to(ar.reshape(1, g * p), (SUBLANES, g * p))
    a_im = jnp.broadcast_to(ai.reshape(1, g * p), (SUBLANES, g * p))
    return bmat.astype(BF16), cmat.astype(BF16), a_re, a_im


def kernel(x, w_in, w_out, lam_qk, subln_g, ssm_lam_re, ssm_lam_im, ssm_log_dt, ssm_b_re, ssm_b_im,
           ssm_c_re, ssm_c_im, ssm_d, glu_w, glu_b, ln1_g, ln1_b, w_ff1, w_ff2, ln2_g, ln2_b):
    bsz, seq, d_model = x.shape
    depth = w_in.shape[0]
    ssm_w = glu_w.shape[1]
    attn_w = w_out.shape[1] - ssm_w
    n_heads = attn_w // V_DIM
    qk_w = n_heads * 2 * HEAD_DIM
    assert bsz == SUBLANES, "the S5 scan keeps the batch on the sublane axis"
    assert w_in.shape[2] == 2 * qk_w + attn_w + ssm_w
    alpha = (2.0 * depth) ** 0.25
    q_scale = math.log2(math.e) / math.sqrt(HEAD_DIM)

    tl = min(512, seq)
    tq = tl
    tc = min(64, seq)
    tf = 1024

    cos, sin, cost, sint = _rope_tables(seq, q_scale)
    for l in range(depth):
        lam_init = 0.8 - 0.6 * math.exp(-0.3 * l)
        w = w_in[l].astype(F32)
        wq = w[:, :qk_w]
        wk = w[:, qk_w:2 * qk_w]
        wv = w[:, 2 * qk_w:2 * qk_w + attn_w]
        wu = w[:, 2 * qk_w + attn_w:]
        qt, k, vt, u = _in_proj(
            x, wq.T.astype(BF16), _rotate_half_columns(wq).T.astype(BF16), wk.astype(BF16),
            _rotate_half_columns(wk).astype(BF16), wv.T.astype(BF16), wu.astype(BF16),
            cos, sin, cost, sint, tl=tl)

        attn = _attention(qt, k, vt, lam_qk[l].astype(F32), subln_g[l].astype(F32),
                          bsz=bsz, n_heads=n_heads, lam_init=lam_init, tq=tq)

        bmat, cmat, a_re, a_im = _ssm_params(ssm_lam_re[l], ssm_lam_im[l], ssm_log_dt[l],
                                             ssm_b_re[l], ssm_b_im[l], ssm_c_re[l], ssm_c_im[l])
        ssm = _ssm(u, bmat, cmat, a_re, a_im,
                   ssm_d[l].astype(F32).reshape(1, ssm_w), glu_w[l].astype(BF16),
                   glu_b[l].astype(F32).reshape(1, ssm_w), tc=tc)

        wo = w_out[l].astype(BF16)
        x = _out_proj(x, attn, ssm, wo[:attn_w], wo[attn_w:], ln1_g[l].astype(F32),
                      ln1_b[l].astype(F32), alpha=alpha, tl=tl)
        x = _ffn(x, w_ff1[l].astype(BF16), w_ff2[l].astype(BF16), ln2_g[l].astype(F32),
                 ln2_b[l].astype(F32), alpha=alpha, tl=min(2 * tl, seq), tf=tf)
    return x
```

```python
import functools
import math

import jax
import jax.numpy as jnp
from jax import lax
from jax.experimental import pallas as pl
from jax.experimental.pallas import tpu as pltpu

F32 = jnp.float32
BF16 = jnp.bfloat16

HEAD_DIM = 64
V_DIM = 2 * HEAD_DIM
SSM_GROUP = 16
SSM_STATE = 64
ROPE_THETA = 10000.0
LN_EPS = 1e-5
RMS_EPS = 1e-5
LANES = 128
SUBLANES = 8
NEG_BIG = -1e30
CHUNK = 128
VMEM_LIMIT = 56 * 1024 * 1024


def _dot(a, b):
    return jnp.dot(a, b, preferred_element_type=F32)


def _layer_norm(y, g, b):
    mu = jnp.mean(y, axis=-1, keepdims=True)
    d = y - mu
    var = jnp.mean(d * d, axis=-1, keepdims=True)
    return d * lax.rsqrt(var + LN_EPS) * g + b


_NT = (((1,), (1,)), ((), ()))


def _in_proj_kernel(x_ref, wqt_ref, wqrt_ref, wk_ref, wkr_ref, wvt_ref, wu_ref,
                    cos_ref, sin_ref, cost_ref, sint_ref, qt_ref, k_ref, vt_ref, u_ref):
    x = x_ref[...].astype(BF16)
    hq = lax.dot_general(wqt_ref[...], x, _NT, preferred_element_type=F32)
    hr = lax.dot_general(wqrt_ref[...], x, _NT, preferred_element_type=F32)
    cost = cost_ref[...]
    sint = sint_ref[...]
    for j in range(hq.shape[0] // LANES):
        sl = slice(j * LANES, (j + 1) * LANES)
        qt_ref[sl, :] = (hq[sl] * cost + hr[sl] * sint).astype(BF16)
    vt_ref[...] = lax.dot_general(wvt_ref[...], x, _NT, preferred_element_type=F32).astype(BF16)
    cos = cos_ref[...]
    sin = sin_ref[...]
    hk = _dot(x, wk_ref[...])
    hkr = _dot(x, wkr_ref[...])
    for j in range(hk.shape[1] // LANES):
        sl = slice(j * LANES, (j + 1) * LANES)
        k_ref[j] = (hk[:, sl] * cos + hkr[:, sl] * sin).astype(BF16)
    u_ref[...] = _dot(x, wu_ref[...])


def _in_proj(x, wqt, wqrt, wk, wkr, wvt, wu, cos, sin, cost, sint, *, tl):
    bsz, seq, d = x.shape
    qw = wqt.shape[0]
    kw = wk.shape[1]
    vw = wvt.shape[0]
    uw = wu.shape[1]
    const = lambda b, t: (0, 0)
    return pl.pallas_call(
        _in_proj_kernel,
        grid=(bsz, seq // tl),
        in_specs=[
            pl.BlockSpec((None, tl, d), lambda b, t: (b, t, 0)),
            pl.BlockSpec(wqt.shape, const),
            pl.BlockSpec(wqrt.shape, const),
            pl.BlockSpec(wk.shape, const),
            pl.BlockSpec(wkr.shape, const),
            pl.BlockSpec(wvt.shape, const),
            pl.BlockSpec(wu.shape, const),
            pl.BlockSpec((tl, LANES), lambda b, t: (t, 0)),
            pl.BlockSpec((tl, LANES), lambda b, t: (t, 0)),
            pl.BlockSpec((LANES, tl), lambda b, t: (0, t)),
            pl.BlockSpec((LANES, tl), lambda b, t: (0, t)),
        ],
        out_specs=[
            pl.BlockSpec((None, qw, tl), lambda b, t: (t, b, 0)),
            pl.BlockSpec((kw // LANES, tl, LANES), lambda b, t: (b, t, 0)),
            pl.BlockSpec((None, vw, tl), lambda b, t: (t, b, 0)),
            pl.BlockSpec((tl, uw), lambda b, t: (t, b)),
        ],
        out_shape=[
            jax.ShapeDtypeStruct((seq // tl, bsz * qw, tl), BF16),
            jax.ShapeDtypeStruct((bsz * kw // LANES, seq, LANES), BF16),
            jax.ShapeDtypeStruct((seq // tl, bsz * vw, tl), BF16),
            jax.ShapeDtypeStruct((seq, bsz * uw), F32),
        ],
        compiler_params=pltpu.CompilerParams(
            dimension_semantics=("parallel", "parallel"), vmem_limit_bytes=VMEM_LIMIT),
        name="in_proj",
    )(x, wqt, wqrt, wk, wkr, wvt, wu, cos, sin, cost, sint)


def _masked_q(qt_ref, qi):
    qt = qt_ref[qi].astype(F32)
    rowid = lax.broadcasted_iota(jnp.int32, qt.shape, 0)
    return (jnp.where(rowid < HEAD_DIM, qt, 0.0).astype(BF16),
            jnp.where(rowid >= HEAD_DIM, qt, 0.0).astype(BF16))


def _scores(mp, kt, qm, k_ref, s_ref, cm_ref):
    tk, tq = s_ref.shape[1:]
    k0 = pl.multiple_of(kt * tk, tk)
    s = _dot(k_ref[pl.ds(k0, tk), :], qm[mp])
    s_ref[mp] = s
    cm_ref[mp] = jnp.max(s.reshape(tk // SUBLANES, SUBLANES, tq), axis=0)


def _attn_q_tile(qi, lq_ref, g_ref, qt_ref, k_ref, vt_ref, o_ref, m_ref, l_ref, acc_ref,
                 s_ref, cm_ref, p_ref, a_ref, *, tq, lam_init):
    qm = _masked_q(qt_ref, qi)
    qm_next = _masked_q(qt_ref, jnp.minimum(qi + 1, qt_ref.shape[0] - 1))
    m_ref[...] = jnp.full(m_ref.shape, NEG_BIG, F32)
    l_ref[...] = jnp.zeros(l_ref.shape, F32)
    acc_ref[...] = jnp.zeros(acc_ref.shape, F32)
    p_ref[...] = jnp.zeros(p_ref.shape, BF16)
    a_ref[...] = jnp.ones(a_ref.shape, F32)

    def softmax_tile(mp, masked):
        if masked:
            cm = jnp.full((SUBLANES, tq), NEG_BIG, F32)
            for c in range(tq // CHUNK):
                sc = s_ref[mp, c * CHUNK:(c + 1) * CHUNK, :]
                row = lax.broadcasted_iota(jnp.int32, sc.shape, 0) + c * CHUNK
                col = lax.broadcasted_iota(jnp.int32, sc.shape, 1)
                sc = jnp.where(row <= col, sc, NEG_BIG)
                s_ref[mp, c * CHUNK:(c + 1) * CHUNK, :] = sc
                cm = jnp.maximum(cm, jnp.max(sc.reshape(CHUNK // SUBLANES, SUBLANES, tq), axis=0))
        else:
            cm = cm_ref[mp]
        m_prev = m_ref[mp]
        m_new = jnp.maximum(m_prev, jnp.max(cm, axis=0, keepdims=True))
        alpha = jnp.exp2(m_prev - m_new)
        m_ref[mp] = m_new
        lsum = jnp.zeros((SUBLANES, tq), F32)
        for c in range(tq // CHUNK):
            pc = jnp.exp2(s_ref[mp, c * CHUNK:(c + 1) * CHUNK, :] - m_new)
            lsum = lsum + jnp.sum(pc.reshape(CHUNK // SUBLANES, SUBLANES, tq), axis=0)
            p_ref[mp, c * CHUNK:(c + 1) * CHUNK, :] = pc.astype(BF16)
        l_ref[mp] = alpha * l_ref[mp] + lsum
        return alpha

    def pv(kt):
        vt = vt_ref[kt]
        for mp in range(2):
            acc_ref[mp] = a_ref[mp] * acc_ref[mp] + _dot(vt, p_ref[mp])

    def step(kt, masked):
        pv(jnp.maximum(kt - 1, 0))
        for mp in range(2):
            a_ref[mp] = softmax_tile(mp, masked)
        for mp in range(2):
            if masked:
                _scores(mp, 0, qm_next, k_ref, s_ref, cm_ref)
            else:
                _scores(mp, kt + 1, qm, k_ref, s_ref, cm_ref)

    def body(i, carry):
        step(2 * i, False)
        step(2 * i + 1, False)
        return carry

    lax.fori_loop(0, qi // 2, body, 0)

    @pl.when(qi % 2 == 1)
    def _():
        step(qi - 1, False)

    step(qi, True)
    pv(qi)

    lq = lq_ref[...]
    lam = (jnp.exp(jnp.sum(lq[0:1] * lq[1:2], axis=1, keepdims=True))
           - jnp.exp(jnp.sum(lq[2:3] * lq[3:4], axis=1, keepdims=True)) + lam_init)
    l0 = jnp.sum(l_ref[0], axis=0, keepdims=True)
    l1 = jnp.sum(l_ref[1], axis=0, keepdims=True)
    ot = acc_ref[0] / l0 - lam * (acc_ref[1] / l1)
    ms = jnp.mean(ot * ot, axis=0, keepdims=True)
    ot = ot * lax.rsqrt(ms + RMS_EPS) * g_ref[...] * (1.0 - lam_init)
    o_ref[pl.ds(pl.multiple_of(qi * tq, tq), tq), :] = ot.T.astype(o_ref.dtype)


def _attn_kernel(*refs, tq, lam_init):
    qt_ref, k_ref = refs[2], refs[3]
    s_ref, cm_ref = refs[9], refs[10]
    n_q = qt_ref.shape[0]
    qm0 = _masked_q(qt_ref, 0)
    for mp in range(2):
        _scores(mp, 0, qm0, k_ref, s_ref, cm_ref)

    def q_tile(qi, carry):
        _attn_q_tile(qi, *refs, tq=tq, lam_init=lam_init)
        return carry

    lax.fori_loop(0, n_q, q_tile, 0)


def _attention(qt, k, vt, lam_qk, subln_g, *, bsz, n_heads, lam_init, tq):
    seq = k.shape[1]
    return pl.pallas_call(
        functools.partial(_attn_kernel, tq=tq, lam_init=lam_init),
        grid=(bsz, n_heads),
        in_specs=[
            pl.BlockSpec(lam_qk.shape, lambda b, h: (0, 0)),
            pl.BlockSpec((V_DIM, 1), lambda b, h: (0, 0)),
            pl.BlockSpec((seq // tq, V_DIM, tq), lambda b, h: (0, b * n_heads + h, 0)),
            pl.BlockSpec((None, seq, V_DIM), lambda b, h: (b * n_heads + h, 0, 0)),
            pl.BlockSpec((seq // tq, V_DIM, tq), lambda b, h: (0, b * n_heads + h, 0)),
        ],
        out_specs=pl.BlockSpec((None, seq, V_DIM), lambda b, h: (b * n_heads + h, 0, 0)),
        out_shape=jax.ShapeDtypeStruct((bsz * n_heads, seq, V_DIM), BF16),
        scratch_shapes=[
            pltpu.VMEM((2, 1, tq), F32),
            pltpu.VMEM((2, SUBLANES, tq), F32),
            pltpu.VMEM((2, V_DIM, tq), F32),
            pltpu.VMEM((2, tq, tq), F32),
            pltpu.VMEM((2, SUBLANES, tq), F32),
            pltpu.VMEM((2, tq, tq), BF16),
            pltpu.VMEM((2, 1, tq), F32),
        ],
        compiler_params=pltpu.CompilerParams(
            dimension_semantics=("parallel", "parallel"), vmem_limit_bytes=VMEM_LIMIT),
        name="diff_attn",
    )(lam_qk, subln_g.reshape(V_DIM, 1), qt, k, vt)


def _gelu_tanh(y):
    c = math.sqrt(2.0 / math.pi)
    return 0.5 * y * (1.0 + jnp.tanh(c * (y + 0.044715 * (y * y * y))))


def _ssm_kernel(u_ref, bmat_ref, cmat_ref, ar_ref, ai_ref, d_ref, gw_ref, gb_ref,
                out_ref, bu_ref, st_ref, *, tc, slab):
    n_half = bmat_ref.shape[0]
    uk = bmat_ref.shape[1]
    hw = bmat_ref.shape[2]
    cw = hw // 2

    @pl.when(pl.program_id(0) == 0)
    def _():
        st_ref[...] = jnp.zeros(st_ref.shape, F32)

    u = pltpu.einshape("t(bc)->(tb)c", u_ref[...], b=SUBLANES)
    ub = u.astype(BF16)
    for hf in range(n_half):
        bu_ref[:, hf * hw:(hf + 1) * hw] = _dot(ub[:, hf * uk:(hf + 1) * uk], bmat_ref[hf])

    for hf in range(n_half):
        for sl in range(cw // slab):
            cr = hf * hw + sl * slab
            ci = cr + cw
            ca = hf * cw + sl * slab
            a_re = ar_ref[:, ca:ca + slab]
            a_im = ai_ref[:, ca:ca + slab]

            def body(t, carry, cr=cr, ci=ci, a_re=a_re, a_im=a_im):
                s_re, s_im = carry
                r0 = pl.multiple_of(t * SUBLANES, SUBLANES)
                n_re = a_re * s_re - a_im * s_im + bu_ref[pl.ds(r0, SUBLANES), cr:cr + slab]
                n_im = a_re * s_im + a_im * s_re + bu_ref[pl.ds(r0, SUBLANES), ci:ci + slab]
                bu_ref[pl.ds(r0, SUBLANES), cr:cr + slab] = n_re
                bu_ref[pl.ds(r0, SUBLANES), ci:ci + slab] = n_im
                return n_re, n_im

            s_re, s_im = lax.fori_loop(
                0, tc, body, (st_ref[:, cr:cr + slab], st_ref[:, ci:ci + slab]), unroll=8)
            st_ref[:, cr:cr + slab] = s_re
            st_ref[:, ci:ci + slab] = s_im

    ys = []
    for hf in range(n_half):
        s = bu_ref[:, hf * hw:(hf + 1) * hw].astype(BF16)
        ys.append(_dot(s, cmat_ref[hf]))
    y = jnp.concatenate(ys, axis=1) + d_ref[...] * u
    y = _gelu_tanh(y)
    z = _dot(y.astype(BF16), gw_ref[...]) + gb_ref[...]
    out = pltpu.einshape("(tb)c->t(bc)", y * jax.nn.sigmoid(z), b=SUBLANES)
    out_ref[...] = out.astype(out_ref.dtype)


def _ssm(u, bmat, cmat, a_re, a_im, d_skip, glu_w, glu_b, *, tc, slab=512):
    seq, bw = u.shape
    blk = tc * SUBLANES
    state_w = bmat.shape[0] * bmat.shape[2]
    c2 = lambda t: (0, 0)
    c3 = lambda t: (0, 0, 0)
    return pl.pallas_call(
        functools.partial(_ssm_kernel, tc=tc, slab=slab),
        grid=(seq // tc,),
        in_specs=[
            pl.BlockSpec((tc, bw), lambda t: (t, 0)),
            pl.BlockSpec(bmat.shape, c3),
            pl.BlockSpec(cmat.shape, c3),
            pl.BlockSpec(a_re.shape, c2),
            pl.BlockSpec(a_im.shape, c2),
            pl.BlockSpec(d_skip.shape, c2),
            pl.BlockSpec(glu_w.shape, c2),
            pl.BlockSpec(glu_b.shape, c2),
        ],
        out_specs=pl.BlockSpec((tc, bw), lambda t: (t, 0)),
        out_shape=jax.ShapeDtypeStruct((seq, bw), BF16),
        scratch_shapes=[
            pltpu.VMEM((blk, state_w), F32),
            pltpu.VMEM((SUBLANES, state_w), F32),
        ],
        compiler_params=pltpu.CompilerParams(
            dimension_semantics=("arbitrary",), vmem_limit_bytes=VMEM_LIMIT),
        name="s5_ssm",
    )(u, bmat, cmat, a_re, a_im, d_skip, glu_w, glu_b)


def _mix_ffn_kernel(x_ref, a_ref, s_ref, wa_ref, ws_ref, g1_ref, b1_ref, w1_ref, w2_ref,
                    g2_ref, b2_ref, o_ref, *, alpha):
    a = jnp.concatenate([a_ref[h] for h in range(a_ref.shape[0])], axis=1)
    mix = _dot(a, wa_ref[...]) + _dot(s_ref[...], ws_ref[...])
    x1 = _layer_norm(alpha * x_ref[...] + mix, g1_ref[...], b1_ref[...])
    h = jnp.maximum(_dot(x1.astype(BF16), w1_ref[...]), 0.0)
    ff = _dot((h * h).astype(BF16), w2_ref[...])
    o_ref[...] = _layer_norm(alpha * x1 + ff, g2_ref[...], b2_ref[...])


def _mix_ffn(x, attn, ssm, wa, ws, g1, b1, w1, w2, g2, b2, *, alpha, tl):
    bsz, seq, d = x.shape
    aw = wa.shape[0]
    sw = ws.shape[0]
    const = lambda bb, t: (0, 0)
    resident = lambda arr: pl.BlockSpec(arr.shape, const, pipeline_mode=pl.Buffered(1))
    row = lambda v: v.reshape(1, d)
    return pl.pallas_call(
        functools.partial(_mix_ffn_kernel, alpha=alpha),
        grid=(bsz, seq // tl),
        in_specs=[
            pl.BlockSpec((None, tl, d), lambda bb, t: (bb, t, 0)),
            pl.BlockSpec((aw // V_DIM, tl, V_DIM), lambda bb, t: (bb, t, 0)),
            pl.BlockSpec((tl, sw), lambda bb, t: (t, bb)),
            resident(wa),
            resident(ws),
            pl.BlockSpec((1, d), const),
            pl.BlockSpec((1, d), const),
            resident(w1),
            resident(w2),
            pl.BlockSpec((1, d), const),
            pl.BlockSpec((1, d), const),
        ],
        out_specs=pl.BlockSpec((None, tl, d), lambda bb, t: (bb, t, 0)),
        out_shape=jax.ShapeDtypeStruct((bsz, seq, d), F32),
        compiler_params=pltpu.CompilerParams(
            dimension_semantics=("parallel", "parallel"), vmem_limit_bytes=VMEM_LIMIT),
        name="mix_ffn_ln",
    )(x, attn, ssm, wa, ws, row(g1), row(b1), w1, w2, row(g2), row(b2))


def _rope_tables(seq, q_scale):
    pos = jnp.arange(seq, dtype=F32)
    inv_freq = ROPE_THETA ** (-jnp.arange(0, HEAD_DIM, 2, dtype=F32) / HEAD_DIM)
    ang = pos[:, None] * inv_freq[None, :]
    reps = LANES // (HEAD_DIM // 2)
    cos = jnp.tile(jnp.cos(ang), (1, reps))
    sin = jnp.tile(jnp.sin(ang), (1, reps))
    return cos, sin, (cos * q_scale).T, (sin * q_scale).T


def _rotate_half_columns(w):
    d, n = w.shape
    w4 = w.reshape(d, n // HEAD_DIM, 2, HEAD_DIM // 2)
    return jnp.stack([-w4[:, :, 1], w4[:, :, 0]], axis=2).reshape(d, n)


def _ssm_params(lam_re, lam_im, log_dt, b_re, b_im, c_re, c_im, n_half=2):
    g, p = lam_re.shape
    c = b_re.shape[-1]
    lr = lam_re.astype(F32)
    li = lam_im.astype(F32)
    dt = jnp.exp(log_dt.astype(F32))[:, None]
    mag = jnp.exp(lr * dt)
    ar = mag * jnp.cos(li * dt)
    ai = mag * jnp.sin(li * dt)
    den = lr * lr + li * li
    fr = ((ar - 1.0) * lr + ai * li) / den
    fi = (ai * lr - (ar - 1.0) * li) / den
    br = b_re.astype(F32)
    bi = b_im.astype(F32)
    bbar_re = fr[..., None] * br - fi[..., None] * bi
    bbar_im = fr[..., None] * bi + fi[..., None] * br
    gh = g // n_half
    eye = jnp.eye(gh, dtype=F32)

    def blockdiag_in(m):
        m = m.reshape(n_half, gh, p, c)
        return jnp.einsum('hgpc,gk->hgckp', m, eye).reshape(n_half, gh * c, gh * p)

    def blockdiag_out(m):
        m = m.reshape(n_half, gh, c, p)
        return jnp.einsum('hgcp,gk->hgpkc', m, eye).reshape(n_half, gh * p, gh * c)

    bmat = jnp.concatenate([blockdiag_in(bbar_re), blockdiag_in(bbar_im)], axis=2)
    cmat = jnp.concatenate([blockdiag_out(c_re.astype(F32)), blockdiag_out(-c_im.astype(F32))], axis=1)
    a_re = jnp.broadcast_to(ar.reshape(1, g * p), (SUBLANES, g * p))
    a_im = jnp.broadcast_to(ai.reshape(1, g * p), (SUBLANES, g * p))
    return bmat.astype(BF16), cmat.astype(BF16), a_re, a_im


def kernel(x, w_in, w_out, lam_qk, subln_g, ssm_lam_re, ssm_lam_im, ssm_log_dt, ssm_b_re, ssm_b_im,
           ssm_c_re, ssm_c_im, ssm_d, glu_w, glu_b, ln1_g, ln1_b, w_ff1, w_ff2, ln2_g, ln2_b):
    bsz, seq, d_model = x.shape
    depth = w_in.shape[0]
    ssm_w = glu_w.shape[1]
    attn_w = w_out.shape[1] - ssm_w
    n_heads = attn_w // V_DIM
    qk_w = n_heads * 2 * HEAD_DIM
    assert bsz == SUBLANES, "the S5 scan keeps the batch on the sublane axis"
    assert w_in.shape[2] == 2 * qk_w + attn_w + ssm_w
    alpha = (2.0 * depth) ** 0.25
    q_scale = math.log2(math.e) / math.sqrt(HEAD_DIM)

    tl = min(512, seq)
    tq = tl
    tc = min(64, seq)

    cos, sin, cost, sint = _rope_tables(seq, q_scale)
    for l in range(depth):
        lam_init = 0.8 - 0.6 * math.exp(-0.3 * l)
        w = w_in[l].astype(F32)
        wq = w[:, :qk_w]
        wk = w[:, qk_w:2 * qk_w]
        wv = w[:, 2 * qk_w:2 * qk_w + attn_w]
        wu = w[:, 2 * qk_w + attn_w:]
        qt, k, vt, u = _in_proj(
            x, wq.T.astype(BF16), _rotate_half_columns(wq).T.astype(BF16), wk.astype(BF16),
            _rotate_half_columns(wk).astype(BF16), wv.T.astype(BF16), wu.astype(BF16),
            cos, sin, cost, sint, tl=tl)

        attn = _attention(qt, k, vt, lam_qk[l].astype(F32), subln_g[l].astype(F32),
                          bsz=bsz, n_heads=n_heads, lam_init=lam_init, tq=tq)

        bmat, cmat, a_re, a_im = _ssm_params(ssm_lam_re[l], ssm_lam_im[l], ssm_log_dt[l],
                                             ssm_b_re[l], ssm_b_im[l], ssm_c_re[l], ssm_c_im[l])
        ssm = _ssm(u, bmat, cmat, a_re, a_im,
                   ssm_d[l].astype(F32).reshape(1, ssm_w), glu_w[l].astype(BF16),
                   glu_b[l].astype(F32).reshape(1, ssm_w), tc=tc)

        wo = w_out[l].astype(BF16)
        x = _mix_ffn(x, attn, ssm, wo[:attn_w], wo[attn_w:], ln1_g[l].astype(F32),
                     ln1_b[l].astype(F32), w_ff1[l].astype(BF16), w_ff2[l].astype(BF16),
                     ln2_g[l].astype(F32), ln2_b[l].astype(F32), alpha=alpha, tl=tl)
    return x
```

```python
import functools
import math

import jax
import jax.numpy as jnp
from jax import lax
from jax.experimental import pallas as pl
from jax.experimental.pallas import tpu as pltpu

F32 = jnp.float32
BF16 = jnp.bfloat16

HEAD_DIM = 64
V_DIM = 2 * HEAD_DIM
SSM_GROUP = 16
SSM_STATE = 64
ROPE_THETA = 10000.0
LN_EPS = 1e-5
RMS_EPS = 1e-5
LANES = 128
SUBLANES = 8
NEG_BIG = -1e30
CHUNK = 128
VMEM_LIMIT = 56 * 1024 * 1024


def _dot(a, b):
    return jnp.dot(a, b, preferred_element_type=F32)


def _layer_norm(y, g, b):
    mu = jnp.mean(y, axis=-1, keepdims=True)
    d = y - mu
    var = jnp.mean(d * d, axis=-1, keepdims=True)
    return d * lax.rsqrt(var + LN_EPS) * g + b


_NT = (((1,), (1,)), ((), ()))


def _rope_rows(h, cos, sin, out_dtype):
    half = HEAD_DIM // 2
    out = []
    for c in range(h.shape[0] // HEAD_DIM):
        x1 = h[c * HEAD_DIM:c * HEAD_DIM + half]
        x2 = h[c * HEAD_DIM + half:(c + 1) * HEAD_DIM]
        out.append((x1 * cos - x2 * sin).astype(out_dtype))
        out.append((x2 * cos + x1 * sin).astype(out_dtype))
    return out


def _in_proj_kernel(x_ref, wqt_ref, wkt_ref, wvt_ref, wu_ref, cosq_ref, sinq_ref, cosk_ref,
                    sink_ref, qt_ref, k_ref, vt_ref, u_ref):
    x = x_ref[...].astype(BF16)
    half = HEAD_DIM // 2
    hq = lax.dot_general(wqt_ref[...], x, _NT, preferred_element_type=F32)
    for i, rows in enumerate(_rope_rows(hq, cosq_ref[...], sinq_ref[...], BF16)):
        qt_ref[i * half:(i + 1) * half, :] = rows
    vt_ref[...] = lax.dot_general(wvt_ref[...], x, _NT, preferred_element_type=F32).astype(BF16)
    hk = lax.dot_general(wkt_ref[...], x, _NT, preferred_element_type=F32)
    kt = jnp.concatenate(_rope_rows(hk, cosk_ref[...], sink_ref[...], F32), axis=0)
    for j in range(k_ref.shape[0]):
        k_ref[j] = kt[j * LANES:(j + 1) * LANES].T.astype(BF16)
    u_ref[...] = _dot(x, wu_ref[...])


def _in_proj(x, wqt, wkt, wvt, wu, cosq, sinq, cosk, sink, *, tl):
    bsz, seq, d = x.shape
    qw = wqt.shape[0]
    kw = wkt.shape[0]
    vw = wvt.shape[0]
    uw = wu.shape[1]
    const = lambda b, t: (0, 0)
    table = pl.BlockSpec((HEAD_DIM // 2, tl), lambda b, t: (0, t))
    return pl.pallas_call(
        _in_proj_kernel,
        grid=(bsz, seq // tl),
        in_specs=[
            pl.BlockSpec((None, tl, d), lambda b, t: (b, t, 0)),
            pl.BlockSpec(wqt.shape, const),
            pl.BlockSpec(wkt.shape, const),
            pl.BlockSpec(wvt.shape, const),
            pl.BlockSpec(wu.shape, const),
            table, table, table, table,
        ],
        out_specs=[
            pl.BlockSpec((None, qw, tl), lambda b, t: (t, b, 0)),
            pl.BlockSpec((kw // LANES, tl, LANES), lambda b, t: (b, t, 0)),
            pl.BlockSpec((None, vw, tl), lambda b, t: (t, b, 0)),
            pl.BlockSpec((tl, uw), lambda b, t: (t, b)),
        ],
        out_shape=[
            jax.ShapeDtypeStruct((seq // tl, bsz * qw, tl), BF16),
            jax.ShapeDtypeStruct((bsz * kw // LANES, seq, LANES), BF16),
            jax.ShapeDtypeStruct((seq // tl, bsz * vw, tl), BF16),
            jax.ShapeDtypeStruct((seq, bsz * uw), F32),
        ],
        compiler_params=pltpu.CompilerParams(
            dimension_semantics=("parallel", "parallel"), vmem_limit_bytes=VMEM_LIMIT),
        name="in_proj",
    )(x, wqt, wkt, wvt, wu, cosq, sinq, cosk, sink)


def _masked_q(qt_ref, qi):
    qt = qt_ref[qi].astype(F32)
    rowid = lax.broadcasted_iota(jnp.int32, qt.shape, 0)
    return (jnp.where(rowid < HEAD_DIM, qt, 0.0).astype(BF16),
            jnp.where(rowid >= HEAD_DIM, qt, 0.0).astype(BF16))


def _scores(mp, kt, qm, k_ref, s_ref, cm_ref):
    tk, tq = s_ref.shape[1:]
    k0 = pl.multiple_of(kt * tk, tk)
    s = _dot(k_ref[pl.ds(k0, tk), :], qm[mp])
    s_ref[mp] = s
    cm_ref[mp] = jnp.max(s.reshape(tk // SUBLANES, SUBLANES, tq), axis=0)


def _attn_q_tile(qi, lq_ref, g_ref, qt_ref, k_ref, vt_ref, o_ref, m_ref, l_ref, acc_ref,
                 s_ref, cm_ref, p_ref, a_ref, *, tq, lam_init):
    qm = _masked_q(qt_ref, qi)
    qm_next = _masked_q(qt_ref, jnp.minimum(qi + 1, qt_ref.shape[0] - 1))
    m_ref[...] = jnp.full(m_ref.shape, NEG_BIG, F32)
    l_ref[...] = jnp.zeros(l_ref.shape, F32)
    acc_ref[...] = jnp.zeros(acc_ref.shape, F32)
    p_ref[...] = jnp.zeros(p_ref.shape, BF16)
    a_ref[...] = jnp.ones(a_ref.shape, F32)

    def softmax_tile(mp, masked):
        if masked:
            cm = jnp.full((SUBLANES, tq), NEG_BIG, F32)
            for c in range(tq // CHUNK):
                sc = s_ref[mp, c * CHUNK:(c + 1) * CHUNK, :]
                row = lax.broadcasted_iota(jnp.int32, sc.shape, 0) + c * CHUNK
                col = lax.broadcasted_iota(jnp.int32, sc.shape, 1)
                sc = jnp.where(row <= col, sc, NEG_BIG)
                s_ref[mp, c * CHUNK:(c + 1) * CHUNK, :] = sc
                cm = jnp.maximum(cm, jnp.max(sc.reshape(CHUNK // SUBLANES, SUBLANES, tq), axis=0))
        else:
            cm = cm_ref[mp]
        m_prev = m_ref[mp]
        m_new = jnp.maximum(m_prev, jnp.max(cm, axis=0, keepdims=True))
        alpha = jnp.exp2(m_prev - m_new)
        m_ref[mp] = m_new
        lsum = jnp.zeros((SUBLANES, tq), F32)
        for c in range(tq // CHUNK):
            pc = jnp.exp2(s_ref[mp, c * CHUNK:(c + 1) * CHUNK, :] - m_new)
            lsum = lsum + jnp.sum(pc.reshape(CHUNK // SUBLANES, SUBLANES, tq), axis=0)
            p_ref[mp, c * CHUNK:(c + 1) * CHUNK, :] = pc.astype(BF16)
        l_ref[mp] = alpha * l_ref[mp] + lsum
        return alpha

    def pv(kt):
        vt = vt_ref[kt]
        for mp in range(2):
            acc_ref[mp] = a_ref[mp] * acc_ref[mp] + _dot(vt, p_ref[mp])

    def step(kt, masked):
        pv(jnp.maximum(kt - 1, 0))
        for mp in range(2):
            a_ref[mp] = softmax_tile(mp, masked)
        for mp in range(2):
            if masked:
                _scores(mp, 0, qm_next, k_ref, s_ref, cm_ref)
            else:
                _scores(mp, kt + 1, qm, k_ref, s_ref, cm_ref)

    def body(i, carry):
        step(2 * i, False)
        step(2 * i + 1, False)
        return carry

    lax.fori_loop(0, qi // 2, body, 0)

    @pl.when(qi % 2 == 1)
    def _():
        step(qi - 1, False)

    step(qi, True)
    pv(qi)

    lq = lq_ref[...]
    lam = (jnp.exp(jnp.sum(lq[0:1] * lq[1:2], axis=1, keepdims=True))
           - jnp.exp(jnp.sum(lq[2:3] * lq[3:4], axis=1, keepdims=True)) + lam_init)
    l0 = jnp.sum(l_ref[0], axis=0, keepdims=True)
    l1 = jnp.sum(l_ref[1], axis=0, keepdims=True)
    ot = acc_ref[0] / l0 - lam * (acc_ref[1] / l1)
    ms = jnp.mean(ot * ot, axis=0, keepdims=True)
    ot = ot * lax.rsqrt(ms + RMS_EPS) * g_ref[...] * (1.0 - lam_init)
    o_ref[pl.ds(pl.multiple_of(qi * tq, tq), tq), :] = ot.T.astype(o_ref.dtype)


def _attn_kernel(*refs, tq, lam_init):
    qt_ref, k_ref = refs[2], refs[3]
    s_ref, cm_ref = refs[9], refs[10]
    n_q = qt_ref.shape[0]
    qm0 = _masked_q(qt_ref, 0)
    for mp in range(2):
        _scores(mp, 0, qm0, k_ref, s_ref, cm_ref)

    def q_tile(qi, carry):
        _attn_q_tile(qi, *refs, tq=tq, lam_init=lam_init)
        return carry

    lax.fori_loop(0, n_q, q_tile, 0)


def _attention(qt, k, vt, lam_qk, subln_g, *, bsz, n_heads, lam_init, tq):
    seq = k.shape[1]
    return pl.pallas_call(
        functools.partial(_attn_kernel, tq=tq, lam_init=lam_init),
        grid=(bsz, n_heads),
        in_specs=[
            pl.BlockSpec(lam_qk.shape, lambda b, h: (0, 0)),
            pl.BlockSpec((V_DIM, 1), lambda b, h: (0, 0)),
            pl.BlockSpec((seq // tq, V_DIM, tq), lambda b, h: (0, b * n_heads + h, 0)),
            pl.BlockSpec((None, seq, V_DIM), lambda b, h: (b * n_heads + h, 0, 0)),
            pl.BlockSpec((seq // tq, V_DIM, tq), lambda b, h: (0, b * n_heads + h, 0)),
        ],
        out_specs=pl.BlockSpec((None, seq, V_DIM), lambda b, h: (b * n_heads + h, 0, 0)),
        out_shape=jax.ShapeDtypeStruct((bsz * n_heads, seq, V_DIM), BF16),
        scratch_shapes=[
            pltpu.VMEM((2, 1, tq), F32),
            pltpu.VMEM((2, SUBLANES, tq), F32),
            pltpu.VMEM((2, V_DIM, tq), F32),
            pltpu.VMEM((2, tq, tq), F32),
            pltpu.VMEM((2, SUBLANES, tq), F32),
            pltpu.VMEM((2, tq, tq), BF16),
            pltpu.VMEM((2, 1, tq), F32),
        ],
        compiler_params=pltpu.CompilerParams(
            dimension_semantics=("parallel", "parallel"), vmem_limit_bytes=VMEM_LIMIT),
        name="diff_attn",
    )(lam_qk, subln_g.reshape(V_DIM, 1), qt, k, vt)


def _gelu_tanh(y):
    c = math.sqrt(2.0 / math.pi)
    return 0.5 * y * (1.0 + jnp.tanh(c * (y + 0.044715 * (y * y * y))))


def _ssm_kernel(u_ref, un_ref, bmat_ref, cmat_ref, ar_ref, ai_ref, d_ref, gw_ref, gb_ref,
                out_ref, urow_ref, bu0_ref, bu1_ref, sb_ref, st_ref, *, tc, slab):
    n_half = bmat_ref.shape[0]
    uk = bmat_ref.shape[1]
    hw = bmat_ref.shape[2]
    cw = hw // 2
    step = pl.program_id(0)

    n_col = 2 * LANES
    bu = (bu0_ref, bu1_ref)

    def relayout(src_ref, slot):
        urow_ref[slot] = pltpu.einshape("t(bc)->(tb)c", src_ref[...], b=SUBLANES)

    def project_tile(slot, j):
        hf, c = divmod(j * n_col, hw)
        ub = urow_ref[slot, :, hf * uk:(hf + 1) * uk].astype(BF16)
        bu[slot][:, hf * hw + c:hf * hw + c + n_col] = _dot(ub, bmat_ref[hf, :, c:c + n_col])

    n_tiles = n_half * hw // n_col

    @pl.when(step == 0)
    def _():
        st_ref[...] = jnp.zeros(st_ref.shape, F32)
        relayout(u_ref, 0)
        for j in range(n_tiles):
            project_tile(0, j)

    def chunk(cur, nxt):
        relayout(un_ref, nxt)
        slabs = [(hf * hw + sl * slab, hf * hw + cw + sl * slab, hf * cw + sl * slab)
                 for hf in range(n_half) for sl in range(cw // slab)]
        state = [(st_ref[:, cr:cr + slab], st_ref[:, ci:ci + slab]) for cr, ci, _ in slabs]
        n_pairs = tc // 2
        for t2 in range(n_pairs):
            for j in range(n_tiles):
                if (j * n_pairs) // n_tiles == t2:
                    project_tile(nxt, j)
            rows2 = slice(2 * t2 * SUBLANES, (2 * t2 + 2) * SUBLANES)
            for i, (cr, ci, ca) in enumerate(slabs):
                a_re = ar_ref[:, ca:ca + slab]
                a_im = ai_ref[:, ca:ca + slab]
                s_re, s_im = state[i]
                pair_re, pair_im = [], []
                for t in (2 * t2, 2 * t2 + 1):
                    rows = slice(t * SUBLANES, (t + 1) * SUBLANES)
                    n_re = a_re * s_re - a_im * s_im + bu[cur][rows, cr:cr + slab]
                    n_im = a_re * s_im + a_im * s_re + bu[cur][rows, ci:ci + slab]
                    s_re, s_im = n_re, n_im
                    pair_re.append(n_re)
                    pair_im.append(n_im)
                state[i] = (s_re, s_im)
                sb_ref[rows2, cr:cr + slab] = jnp.concatenate(pair_re, axis=0).astype(BF16)
                sb_ref[rows2, ci:ci + slab] = jnp.concatenate(pair_im, axis=0).astype(BF16)
        for (cr, ci, _), (s_re, s_im) in zip(slabs, state):
            st_ref[:, cr:cr + slab] = s_re
            st_ref[:, ci:ci + slab] = s_im

        ys = [_dot(sb_ref[:, hf * hw:(hf + 1) * hw], cmat_ref[hf]) for hf in range(n_half)]
        y = jnp.concatenate(ys, axis=1) + d_ref[...] * urow_ref[cur]
        y = _gelu_tanh(y)
        z = _dot(y.astype(BF16), gw_ref[...]) + gb_ref[...]
        out = pltpu.einshape("(tb)c->t(bc)", y * jax.nn.sigmoid(z), b=SUBLANES)
        out_ref[...] = out.astype(out_ref.dtype)

    @pl.when(step % 2 == 0)
    def _():
        chunk(0, 1)

    @pl.when(step % 2 == 1)
    def _():
        chunk(1, 0)


def _ssm(u, bmat, cmat, a_re, a_im, d_skip, glu_w, glu_b, *, tc, slab=512):
    seq, bw = u.shape
    blk = tc * SUBLANES
    state_w = bmat.shape[0] * bmat.shape[2]
    n_chunks = seq // tc
    c2 = lambda t: (0, 0)
    c3 = lambda t: (0, 0, 0)
    return pl.pallas_call(
        functools.partial(_ssm_kernel, tc=tc, slab=slab),
        grid=(n_chunks,),
        in_specs=[
            pl.BlockSpec((tc, bw), lambda t: (t, 0)),
            pl.BlockSpec((tc, bw), lambda t: (jnp.minimum(t + 1, n_chunks - 1), 0)),
            pl.BlockSpec(bmat.shape, c3),
            pl.BlockSpec(cmat.shape, c3),
            pl.BlockSpec(a_re.shape, c2),
            pl.BlockSpec(a_im.shape, c2),
            pl.BlockSpec(d_skip.shape, c2),
            pl.BlockSpec(glu_w.shape, c2),
            pl.BlockSpec(glu_b.shape, c2),
        ],
        out_specs=pl.BlockSpec((tc, bw), lambda t: (t, 0)),
        out_shape=jax.ShapeDtypeStruct((seq, bw), BF16),
        scratch_shapes=[
            pltpu.VMEM((2, blk, bw // SUBLANES), F32),
            pltpu.VMEM((blk, state_w), F32),
            pltpu.VMEM((blk, state_w), F32),
            pltpu.VMEM((blk, state_w), BF16),
            pltpu.VMEM((SUBLANES, state_w), F32),
        ],
        compiler_params=pltpu.CompilerParams(
            dimension_semantics=("arbitrary",), vmem_limit_bytes=VMEM_LIMIT),
        name="s5_ssm",
    )(u, u, bmat, cmat, a_re, a_im, d_skip, glu_w, glu_b)


def _mix_ffn_kernel(x_ref, a_ref, s_ref, wa_ref, ws_ref, g1_ref, b1_ref, w1_ref, w2_ref,
                    g2_ref, b2_ref, o_ref, *, alpha):
    a = jnp.concatenate([a_ref[h] for h in range(a_ref.shape[0])], axis=1)
    mix = _dot(a, wa_ref[...]) + _dot(s_ref[...], ws_ref[...])
    x1 = _layer_norm(alpha * x_ref[...] + mix, g1_ref[...], b1_ref[...])
    h = jnp.maximum(_dot(x1.astype(BF16), w1_ref[...]), 0.0)
    ff = _dot((h * h).astype(BF16), w2_ref[...])
    o_ref[...] = _layer_norm(alpha * x1 + ff, g2_ref[...], b2_ref[...])


def _mix_ffn(x, attn, ssm, wa, ws, g1, b1, w1, w2, g2, b2, *, alpha, tl):
    bsz, seq, d = x.shape
    aw = wa.shape[0]
    sw = ws.shape[0]
    const = lambda bb, t: (0, 0)
    resident = lambda arr: pl.BlockSpec(arr.shape, const, pipeline_mode=pl.Buffered(1))
    row = lambda v: v.reshape(1, d)
    return pl.pallas_call(
        functools.partial(_mix_ffn_kernel, alpha=alpha),
        grid=(bsz, seq // tl),
        in_specs=[
            pl.BlockSpec((None, tl, d), lambda bb, t: (bb, t, 0)),
            pl.BlockSpec((aw // V_DIM, tl, V_DIM), lambda bb, t: (bb, t, 0)),
            pl.BlockSpec((tl, sw), lambda bb, t: (t, bb)),
            resident(wa),
            resident(ws),
            pl.BlockSpec((1, d), const),
            pl.BlockSpec((1, d), const),
            resident(w1),
            resident(w2),
            pl.BlockSpec((1, d), const),
            pl.BlockSpec((1, d), const),
        ],
        out_specs=pl.BlockSpec((None, tl, d), lambda bb, t: (bb, t, 0)),
        out_shape=jax.ShapeDtypeStruct((bsz, seq, d), F32),
        compiler_params=pltpu.CompilerParams(
            dimension_semantics=("parallel", "parallel"), vmem_limit_bytes=VMEM_LIMIT),
        name="mix_ffn_ln",
    )(x, attn, ssm, wa, ws, row(g1), row(b1), w1, w2, row(g2), row(b2))


def _rope_tables(seq, q_scale):
    pos = jnp.arange(seq, dtype=F32)
    inv_freq = ROPE_THETA ** (-jnp.arange(0, HEAD_DIM, 2, dtype=F32) / HEAD_DIM)
    ang = inv_freq[:, None] * pos[None, :]
    cos = jnp.cos(ang)
    sin = jnp.sin(ang)
    return cos * q_scale, sin * q_scale, cos, sin


def _ssm_params(lam_re, lam_im, log_dt, b_re, b_im, c_re, c_im, n_half=2):
    g, p = lam_re.shape
    c = b_re.shape[-1]
    lr = lam_re.astype(F32)
    li = lam_im.astype(F32)
    dt = jnp.exp(log_dt.astype(F32))[:, None]
    mag = jnp.exp(lr * dt)
    ar = mag * jnp.cos(li * dt)
    ai = mag * jnp.sin(li * dt)
    den = lr * lr + li * li
    fr = ((ar - 1.0) * lr + ai * li) / den
    fi = (ai * lr - (ar - 1.0) * li) / den
    br = b_re.astype(F32)
    bi = b_im.astype(F32)
    bbar_re = fr[..., None] * br - fi[..., None] * bi
    bbar_im = fr[..., None] * bi + fi[..., None] * br
    gh = g // n_half
    eye = jnp.eye(gh, dtype=F32)

    def blockdiag_in(m):
        m = m.reshape(n_half, gh, p, c)
        return jnp.einsum('hgpc,gk->hgckp', m, eye).reshape(n_half, gh * c, gh * p)

    def blockdiag_out(m):
        m = m.reshape(n_half, gh, c, p)
        return jnp.einsum('hgcp,gk->hgpkc', m, eye).reshape(n_half, gh * p, gh * c)

    bmat = jnp.concatenate([blockdiag_in(bbar_re), blockdiag_in(bbar_im)], axis=2)
    cmat = jnp.concatenate([blockdiag_out(c_re.astype(F32)), blockdiag_out(-c_im.astype(F32))], axis=1)
    a_re = jnp.broadcast_to(ar.reshape(1, g * p), (SUBLANES, g * p))
    a_im = jnp.broadcast_to(ai.reshape(1, g * p), (SUBLANES, g * p))
    return bmat.astype(BF16), cmat.astype(BF16), a_re, a_im


def kernel(x, w_in, w_out, lam_qk, subln_g, ssm_lam_re, ssm_lam_im, ssm_log_dt, ssm_b_re, ssm_b_im,
           ssm_c_re, ssm_c_im, ssm_d, glu_w, glu_b, ln1_g, ln1_b, w_ff1, w_ff2, ln2_g, ln2_b):
    bsz, seq, d_model = x.shape
    depth = w_in.shape[0]
    ssm_w = glu_w.shape[1]
    attn_w = w_out.shape[1] - ssm_w
    n_heads = attn_w // V_DIM
    qk_w = n_heads * 2 * HEAD_DIM
    assert bsz == SUBLANES, "the S5 scan keeps the batch on the sublane axis"
    assert w_in.shape[2] == 2 * qk_w + attn_w + ssm_w
    alpha = (2.0 * depth) ** 0.25
    q_scale = math.log2(math.e) / math.sqrt(HEAD_DIM)

    tl = min(512, seq)
    tq = tl
    tc = min(64, seq)

    cosq, sinq, cosk, sink = _rope_tables(seq, q_scale)
    for l in range(depth):
        lam_init = 0.8 - 0.6 * math.exp(-0.3 * l)
        w = w_in[l].astype(BF16)
        wq = w[:, :qk_w]
        wk = w[:, qk_w:2 * qk_w]
        wv = w[:, 2 * qk_w:2 * qk_w + attn_w]
        wu = w[:, 2 * qk_w + attn_w:]
        qt, k, vt, u = _in_proj(x, wq.T, wk.T, wv.T, wu, cosq, sinq, cosk, sink, tl=tl)

        attn = _attention(qt, k, vt, lam_qk[l].astype(F32), subln_g[l].astype(F32),
                          bsz=bsz, n_heads=n_heads, lam_init=lam_init, tq=tq)

        bmat, cmat, a_re, a_im = _ssm_params(ssm_lam_re[l], ssm_lam_im[l], ssm_log_dt[l],
                                             ssm_b_re[l], ssm_b_im[l], ssm_c_re[l], ssm_c_im[l])
        ssm = _ssm(u, bmat, cmat, a_re, a_im,
                   ssm_d[l].astype(F32).reshape(1, ssm_w), glu_w[l].astype(BF16),
                   glu_b[l].astype(F32).reshape(1, ssm_w), tc=tc)

        wo = w_out[l].astype(BF16)
        x = _mix_ffn(x, attn, ssm, wo[:attn_w], wo[attn_w:], ln1_g[l].astype(F32),
                     ln1_b[l].astype(F32), w_ff1[l].astype(BF16), w_ff2[l].astype(BF16),
                     ln2_g[l].astype(F32), ln2_b[l].astype(F32), alpha=alpha, tl=tl)
    return x
```

```python
import functools
import math

import jax
import jax.numpy as jnp
from jax import lax
from jax.experimental import pallas as pl
from jax.experimental.pallas import tpu as pltpu

F32 = jnp.float32
BF16 = jnp.bfloat16

HEAD_DIM = 64
V_DIM = 2 * HEAD_DIM
SSM_GROUP = 16
SSM_STATE = 64
ROPE_THETA = 10000.0
LN_EPS = 1e-5
RMS_EPS = 1e-5
LANES = 128
SUBLANES = 8
NEG_BIG = -1e30
CHUNK = 128
KV_UNROLL = 2
VMEM_LIMIT = 56 * 1024 * 1024


def _dot(a, b):
    return jnp.dot(a, b, preferred_element_type=F32)


def _layer_norm(y, g, b):
    mu = jnp.mean(y, axis=-1, keepdims=True)
    d = y - mu
    var = jnp.mean(d * d, axis=-1, keepdims=True)
    return d * lax.rsqrt(var + LN_EPS) * g + b


_NT = (((1,), (1,)), ((), ()))


def _rope_rows(h, cos, sin, out_dtype):
    half = HEAD_DIM // 2
    out = []
    for c in range(h.shape[0] // HEAD_DIM):
        x1 = h[c * HEAD_DIM:c * HEAD_DIM + half]
        x2 = h[c * HEAD_DIM + half:(c + 1) * HEAD_DIM]
        out.append((x1 * cos - x2 * sin).astype(out_dtype))
        out.append((x2 * cos + x1 * sin).astype(out_dtype))
    return out


def _in_proj_kernel(x_ref, wqt_ref, wkt_ref, wvt_ref, wu_ref, cosq_ref, sinq_ref, cosk_ref,
                    sink_ref, qt_ref, k_ref, vt_ref, u_ref):
    x = x_ref[...].astype(BF16)
    half = HEAD_DIM // 2
    hq = lax.dot_general(wqt_ref[...], x, _NT, preferred_element_type=F32)
    q_rows = _rope_rows(hq, cosq_ref[...], sinq_ref[...], BF16)
    zeros = jnp.zeros((HEAD_DIM, x.shape[0]), BF16)
    for c in range(len(q_rows) // 2):
        mp = c % 2
        data = c * V_DIM + mp * HEAD_DIM
        qt_ref[data:data + half, :] = q_rows[2 * c]
        qt_ref[data + half:data + HEAD_DIM, :] = q_rows[2 * c + 1]
        other = c * V_DIM + (1 - mp) * HEAD_DIM
        qt_ref[other:other + HEAD_DIM, :] = zeros
    vt_ref[...] = lax.dot_general(wvt_ref[...], x, _NT, preferred_element_type=F32).astype(BF16)
    hk = lax.dot_general(wkt_ref[...], x, _NT, preferred_element_type=F32)
    kt = jnp.concatenate(_rope_rows(hk, cosk_ref[...], sink_ref[...], F32), axis=0)
    for j in range(k_ref.shape[0]):
        k_ref[j] = kt[j * LANES:(j + 1) * LANES].T.astype(BF16)
    u_ref[...] = _dot(x, wu_ref[...])


def _in_proj(x, wqt, wkt, wvt, wu, cosq, sinq, cosk, sink, *, tl):
    bsz, seq, d = x.shape
    qw = wqt.shape[0]
    kw = wkt.shape[0]
    vw = wvt.shape[0]
    uw = wu.shape[1]
    const = lambda b, t: (0, 0)
    table = pl.BlockSpec((HEAD_DIM // 2, tl), lambda b, t: (0, t))
    return pl.pallas_call(
        _in_proj_kernel,
        grid=(bsz, seq // tl),
        in_specs=[
            pl.BlockSpec((None, tl, d), lambda b, t: (b, t, 0)),
            pl.BlockSpec(wqt.shape, const),
            pl.BlockSpec(wkt.shape, const),
            pl.BlockSpec(wvt.shape, const),
            pl.BlockSpec(wu.shape, const),
            table, table, table, table,
        ],
        out_specs=[
            pl.BlockSpec((None, 2 * qw, tl), lambda b, t: (t, b, 0)),
            pl.BlockSpec((kw // LANES, tl, LANES), lambda b, t: (b, t, 0)),
            pl.BlockSpec((None, vw, tl), lambda b, t: (t, b, 0)),
            pl.BlockSpec((tl, uw), lambda b, t: (t, b)),
        ],
        out_shape=[
            jax.ShapeDtypeStruct((seq // tl, bsz * 2 * qw, tl), BF16),
            jax.ShapeDtypeStruct((bsz * kw // LANES, seq, LANES), BF16),
            jax.ShapeDtypeStruct((seq // tl, bsz * vw, tl), BF16),
            jax.ShapeDtypeStruct((seq, bsz * uw), F32),
        ],
        compiler_params=pltpu.CompilerParams(
            dimension_semantics=("parallel", "parallel"), vmem_limit_bytes=VMEM_LIMIT),
        name="in_proj",
    )(x, wqt, wkt, wvt, wu, cosq, sinq, cosk, sink)


def _scores(mp, kt, qi, qt_ref, k_ref, s_ref, cm_ref):
    tk, tq = s_ref.shape[1:]
    k0 = pl.multiple_of(kt * tk, tk)
    s = _dot(k_ref[pl.ds(k0, tk), :], qt_ref[qi, mp * V_DIM:(mp + 1) * V_DIM, :])
    s_ref[mp] = s
    cm_ref[mp] = jnp.max(s.reshape(tk // SUBLANES, SUBLANES, tq), axis=0)


def _attn_q_tile(qi, lq_ref, g_ref, qt_ref, k_ref, vt_ref, o_ref, m_ref, l_ref, acc_ref,
                 s_ref, cm_ref, p_ref, a_ref, *, tq, lam_init):
    qi_next = jnp.minimum(qi + 1, qt_ref.shape[0] - 1)
    m_ref[...] = jnp.full(m_ref.shape, NEG_BIG, F32)
    l_ref[...] = jnp.zeros(l_ref.shape, F32)
    acc_ref[...] = jnp.zeros(acc_ref.shape, F32)
    p_ref[...] = jnp.zeros(p_ref.shape, BF16)
    a_ref[...] = jnp.ones(a_ref.shape, F32)

    def softmax_tile(mp, masked):
        if masked:
            cm = jnp.full((SUBLANES, tq), NEG_BIG, F32)
            for c in range(tq // CHUNK):
                sc = s_ref[mp, c * CHUNK:(c + 1) * CHUNK, :]
                row = lax.broadcasted_iota(jnp.int32, sc.shape, 0) + c * CHUNK
                col = lax.broadcasted_iota(jnp.int32, sc.shape, 1)
                sc = jnp.where(row <= col, sc, NEG_BIG)
                s_ref[mp, c * CHUNK:(c + 1) * CHUNK, :] = sc
                cm = jnp.maximum(cm, jnp.max(sc.reshape(CHUNK // SUBLANES, SUBLANES, tq), axis=0))
        else:
            cm = cm_ref[mp]
        m_prev = m_ref[mp]
        m_new = jnp.maximum(m_prev, jnp.max(cm, axis=0, keepdims=True))
        alpha = jnp.exp2(m_prev - m_new)
        m_ref[mp] = m_new
        lsum = jnp.zeros((SUBLANES, tq), F32)
        for c in range(tq // CHUNK):
            pc = jnp.exp2(s_ref[mp, c * CHUNK:(c + 1) * CHUNK, :] - m_new)
            lsum = lsum + jnp.sum(pc.reshape(CHUNK // SUBLANES, SUBLANES, tq), axis=0)
            p_ref[mp, c * CHUNK:(c + 1) * CHUNK, :] = pc.astype(BF16)
        l_ref[mp] = alpha * l_ref[mp] + lsum
        return alpha

    def pv(kt):
        vt = vt_ref[kt]
        for mp in range(2):
            acc_ref[mp] = a_ref[mp] * acc_ref[mp] + _dot(vt, p_ref[mp])

    def step(kt, masked):
        pv(jnp.maximum(kt - 1, 0))
        for mp in range(2):
            a_ref[mp] = softmax_tile(mp, masked)
        for mp in range(2):
            if masked:
                _scores(mp, 0, qi_next, qt_ref, k_ref, s_ref, cm_ref)
            else:
                _scores(mp, kt + 1, qi, qt_ref, k_ref, s_ref, cm_ref)

    def body(i, carry):
        for j in range(KV_UNROLL):
            step(KV_UNROLL * i + j, False)
        return carry

    def tail(kt, carry):
        step(kt, False)
        return carry

    n_main = qi // KV_UNROLL
    lax.fori_loop(0, n_main, body, 0)
    lax.fori_loop(n_main * KV_UNROLL, qi, tail, 0)

    step(qi, True)
    pv(qi)

    lq = lq_ref[...]
    lam = (jnp.exp(jnp.sum(lq[0:1] * lq[1:2], axis=1, keepdims=True))
           - jnp.exp(jnp.sum(lq[2:3] * lq[3:4], axis=1, keepdims=True)) + lam_init)
    l0 = jnp.sum(l_ref[0], axis=0, keepdims=True)
    l1 = jnp.sum(l_ref[1], axis=0, keepdims=True)
    ot = acc_ref[0] / l0 - lam * (acc_ref[1] / l1)
    ms = jnp.mean(ot * ot, axis=0, keepdims=True)
    ot = ot * lax.rsqrt(ms + RMS_EPS) * g_ref[...] * (1.0 - lam_init)
    o_ref[pl.ds(pl.multiple_of(qi * tq, tq), tq), :] = ot.T.astype(o_ref.dtype)


def _attn_kernel(*refs, tq, lam_init):
    qt_ref, k_ref = refs[2], refs[3]
    s_ref, cm_ref = refs[9], refs[10]
    n_q = qt_ref.shape[0]
    for mp in range(2):
        _scores(mp, 0, 0, qt_ref, k_ref, s_ref, cm_ref)

    def q_tile(qi, carry):
        _attn_q_tile(qi, *refs, tq=tq, lam_init=lam_init)
        return carry

    lax.fori_loop(0, n_q, q_tile, 0)


def _attention(qt, k, vt, lam_qk, subln_g, *, bsz, n_heads, lam_init, tq):
    seq = k.shape[1]
    return pl.pallas_call(
        functools.partial(_attn_kernel, tq=tq, lam_init=lam_init),
        grid=(bsz, n_heads),
        in_specs=[
            pl.BlockSpec(lam_qk.shape, lambda b, h: (0, 0)),
            pl.BlockSpec((V_DIM, 1), lambda b, h: (0, 0)),
            pl.BlockSpec((seq // tq, 2 * V_DIM, tq), lambda b, h: (0, b * n_heads + h, 0)),
            pl.BlockSpec((None, seq, V_DIM), lambda b, h: (b * n_heads + h, 0, 0)),
            pl.BlockSpec((seq // tq, V_DIM, tq), lambda b, h: (0, b * n_heads + h, 0)),
        ],
        out_specs=pl.BlockSpec((None, seq, V_DIM), lambda b, h: (b * n_heads + h, 0, 0)),
        out_shape=jax.ShapeDtypeStruct((bsz * n_heads, seq, V_DIM), BF16),
        scratch_shapes=[
            pltpu.VMEM((2, 1, tq), F32),
            pltpu.VMEM((2, SUBLANES, tq), F32),
            pltpu.VMEM((2, V_DIM, tq), F32),
            pltpu.VMEM((2, tq, tq), F32),
            pltpu.VMEM((2, SUBLANES, tq), F32),
            pltpu.VMEM((2, tq, tq), BF16),
            pltpu.VMEM((2, 1, tq), F32),
        ],
        compiler_params=pltpu.CompilerParams(
            dimension_semantics=("parallel", "parallel"), vmem_limit_bytes=VMEM_LIMIT),
        name="diff_attn",
    )(lam_qk, subln_g.reshape(V_DIM, 1), qt, k, vt)


def _gelu_tanh(y):
    c = math.sqrt(2.0 / math.pi)
    return 0.5 * y * (1.0 + jnp.tanh(c * (y + 0.044715 * (y * y * y))))


def _anchor(x):
    bits = pltpu.bitcast(x, jnp.uint32)
    return pltpu.bitcast(lax.shift_right_logical(bits, jnp.uint32(32)), F32)


def _ssm_kernel(u_ref, un_ref, bmat_ref, cmat_ref, ar_ref, ai_ref, d_ref, gw_ref, gb_ref,
                out_ref, urow_ref, ubf_ref, bu0_ref, bu1_ref, sb_ref, st_ref, *, tc, slab):
    n_half = bmat_ref.shape[0]
    uk = bmat_ref.shape[1]
    hw = bmat_ref.shape[2]
    cw = hw // 2
    step = pl.program_id(0)

    n_col = 2 * LANES
    bu = (bu0_ref, bu1_ref)

    def relayout(src_ref, slot):
        u = pltpu.einshape("t(bc)->(tb)c", src_ref[...], b=SUBLANES)
        urow_ref[slot] = u
        ubf_ref[slot] = u.astype(BF16)

    def project_tile(slot, j):
        hf, c = divmod(j * n_col, hw)
        bu[slot][:, hf * hw + c:hf * hw + c + n_col] = _dot(
            ubf_ref[slot, :, hf * uk:(hf + 1) * uk], bmat_ref[hf, :, c:c + n_col])

    n_tiles = n_half * hw // n_col

    @pl.when(step == 0)
    def _():
        st_ref[...] = jnp.zeros(st_ref.shape, F32)
        relayout(u_ref, 0)
        for j in range(n_tiles):
            project_tile(0, j)

    def chunk(cur, nxt):
        relayout(un_ref, nxt)
        slabs = [(hf * hw + sl * slab, hf * hw + cw + sl * slab, hf * cw + sl * slab)
                 for hf in range(n_half) for sl in range(cw // slab)]
        state = [(st_ref[:, cr:cr + slab], st_ref[:, ci:ci + slab]) for cr, ci, _ in slabs]
        n_pairs = tc // 2
        tile_at = {(j * n_pairs) // n_tiles: j for j in range(n_tiles)} if n_pairs >= n_tiles else {}
        for t2 in range(n_pairs):
            j = tile_at.get(t2)
            if j is not None:
                hf_j, c_j = divmod(j * n_col, hw)
                if j > 0:
                    z = _anchor(state[0][0][:, :LANES])
                    head = (slice(0, 2 * SUBLANES), slice(hf_j * uk, hf_j * uk + LANES))
                    ubf_ref[(nxt,) + head] = ubf_ref[(nxt,) + head] + jnp.concatenate([z, z], 0).astype(BF16)
                project_tile(nxt, j)
                if j > 0:
                    hf_p, c_p = divmod((j - 1) * n_col, hw)
                    z = _anchor(bu[nxt][0:SUBLANES, hf_p * hw + c_p:hf_p * hw + c_p + LANES])
                    zz = jnp.concatenate([z] * (slab // LANES), axis=1)
                    state = [(s_re + zz, s_im) for s_re, s_im in state]
            rows2 = slice(2 * t2 * SUBLANES, (2 * t2 + 2) * SUBLANES)
            for i, (cr, ci, ca) in enumerate(slabs):
                a_re = ar_ref[:, ca:ca + slab]
                a_im = ai_ref[:, ca:ca + slab]
                s_re, s_im = state[i]
                pair_re, pair_im = [], []
                for t in (2 * t2, 2 * t2 + 1):
                    rows = slice(t * SUBLANES, (t + 1) * SUBLANES)
                    n_re = a_re * s_re - a_im * s_im + bu[cur][rows, cr:cr + slab]
                    n_im = a_re * s_im + a_im * s_re + bu[cur][rows, ci:ci + slab]
                    s_re, s_im = n_re, n_im
                    pair_re.append(n_re)
                    pair_im.append(n_im)
                state[i] = (s_re, s_im)
                sb_ref[rows2, cr:cr + slab] = jnp.concatenate(pair_re, axis=0).astype(BF16)
                sb_ref[rows2, ci:ci + slab] = jnp.concatenate(pair_im, axis=0).astype(BF16)
        for (cr, ci, _), (s_re, s_im) in zip(slabs, state):
            st_ref[:, cr:cr + slab] = s_re
            st_ref[:, ci:ci + slab] = s_im
        if not tile_at:
            for j in range(n_tiles):
                project_tile(nxt, j)

        ys = [_dot(sb_ref[:, hf * hw:(hf + 1) * hw], cmat_ref[hf]) for hf in range(n_half)]
        y = jnp.concatenate(ys, axis=1) + d_ref[...] * urow_ref[cur]
        y = _gelu_tanh(y)
        z = _dot(y.astype(BF16), gw_ref[...]) + gb_ref[...]
        out = pltpu.einshape("(tb)c->t(bc)", y * jax.nn.sigmoid(z), b=SUBLANES)
        out_ref[...] = out.astype(out_ref.dtype)

    @pl.when(step % 2 == 0)
    def _():
        chunk(0, 1)

    @pl.when(step % 2 == 1)
    def _():
        chunk(1, 0)


def _ssm(u, bmat, cmat, a_re, a_im, d_skip, glu_w, glu_b, *, tc, slab=512):
    seq, bw = u.shape
    blk = tc * SUBLANES
    state_w = bmat.shape[0] * bmat.shape[2]
    n_chunks = seq // tc
    c2 = lambda t: (0, 0)
    c3 = lambda t: (0, 0, 0)
    return pl.pallas_call(
        functools.partial(_ssm_kernel, tc=tc, slab=slab),
        grid=(n_chunks,),
        in_specs=[
            pl.BlockSpec((tc, bw), lambda t: (t, 0)),
            pl.BlockSpec((tc, bw), lambda t: (jnp.minimum(t + 1, n_chunks - 1), 0)),
            pl.BlockSpec(bmat.shape, c3),
            pl.BlockSpec(cmat.shape, c3),
            pl.BlockSpec(a_re.shape, c2),
            pl.BlockSpec(a_im.shape, c2),
            pl.BlockSpec(d_skip.shape, c2),
            pl.BlockSpec(glu_w.shape, c2),
            pl.BlockSpec(glu_b.shape, c2),
        ],
        out_specs=pl.BlockSpec((tc, bw), lambda t: (t, 0)),
        out_shape=jax.ShapeDtypeStruct((seq, bw), BF16),
        scratch_shapes=[
            pltpu.VMEM((2, blk, bw // SUBLANES), F32),
            pltpu.VMEM((2, blk, bw // SUBLANES), BF16),
            pltpu.VMEM((blk, state_w), F32),
            pltpu.VMEM((blk, state_w), F32),
            pltpu.VMEM((blk, state_w), BF16),
            pltpu.VMEM((SUBLANES, state_w), F32),
        ],
        compiler_params=pltpu.CompilerParams(
            dimension_semantics=("arbitrary",), vmem_limit_bytes=VMEM_LIMIT),
        name="s5_ssm",
    )(u, u, bmat, cmat, a_re, a_im, d_skip, glu_w, glu_b)


def _mix_ffn_kernel(x_ref, a_ref, s_ref, wa_ref, ws_ref, g1_ref, b1_ref, w1_ref, w2_ref,
                    g2_ref, b2_ref, o_ref, *, alpha):
    a = jnp.concatenate([a_ref[h] for h in range(a_ref.shape[0])], axis=1)
    mix = _dot(a, wa_ref[...]) + _dot(s_ref[...], ws_ref[...])
    x1 = _layer_norm(alpha * x_ref[...] + mix, g1_ref[...], b1_ref[...])
    h = jnp.maximum(_dot(x1.astype(BF16), w1_ref[...]), 0.0)
    ff = _dot((h * h).astype(BF16), w2_ref[...])
    o_ref[...] = _layer_norm(alpha * x1 + ff, g2_ref[...], b2_ref[...])


def _mix_ffn(x, attn, ssm, wa, ws, g1, b1, w1, w2, g2, b2, *, alpha, tl):
    bsz, seq, d = x.shape
    aw = wa.shape[0]
    sw = ws.shape[0]
    const = lambda bb, t: (0, 0)
    resident = lambda arr: pl.BlockSpec(arr.shape, const, pipeline_mode=pl.Buffered(1))
    row = lambda v: v.reshape(1, d)
    return pl.pallas_call(
        functools.partial(_mix_ffn_kernel, alpha=alpha),
        grid=(bsz, seq // tl),
        in_specs=[
            pl.BlockSpec((None, tl, d), lambda bb, t: (bb, t, 0)),
            pl.BlockSpec((aw // V_DIM, tl, V_DIM), lambda bb, t: (bb, t, 0)),
            pl.BlockSpec((tl, sw), lambda bb, t: (t, bb)),
            resident(wa),
            resident(ws),
            pl.BlockSpec((1, d), const),
            pl.BlockSpec((1, d), const),
            resident(w1),
            resident(w2),
            pl.BlockSpec((1, d), const),
            pl.BlockSpec((1, d), const),
        ],
        out_specs=pl.BlockSpec((None, tl, d), lambda bb, t: (bb, t, 0)),
        out_shape=jax.ShapeDtypeStruct((bsz, seq, d), F32),
        compiler_params=pltpu.CompilerParams(
            dimension_semantics=("parallel", "parallel"), vmem_limit_bytes=VMEM_LIMIT),
        name="mix_ffn_ln",
    )(x, attn, ssm, wa, ws, row(g1), row(b1), w1, w2, row(g2), row(b2))


def _rope_tables(seq, q_scale):
    pos = jnp.arange(seq, dtype=F32)
    inv_freq = ROPE_THETA ** (-jnp.arange(0, HEAD_DIM, 2, dtype=F32) / HEAD_DIM)
    ang = inv_freq[:, None] * pos[None, :]
    cos = jnp.cos(ang)
    sin = jnp.sin(ang)
    return cos * q_scale, sin * q_scale, cos, sin


def _ssm_params(lam_re, lam_im, log_dt, b_re, b_im, c_re, c_im, n_half=2):
    g, p = lam_re.shape
    c = b_re.shape[-1]
    lr = lam_re.astype(F32)
    li = lam_im.astype(F32)
    dt = jnp.exp(log_dt.astype(F32))[:, None]
    mag = jnp.exp(lr * dt)
    ar = mag * jnp.cos(li * dt)
    ai = mag * jnp.sin(li * dt)
    den = lr * lr + li * li
    fr = ((ar - 1.0) * lr + ai * li) / den
    fi = (ai * lr - (ar - 1.0) * li) / den
    br = b_re.astype(F32)
    bi = b_im.astype(F32)
    bbar_re = fr[..., None] * br - fi[..., None] * bi
    bbar_im = fr[..., None] * bi + fi[..., None] * br
    gh = g // n_half
    eye = jnp.eye(gh, dtype=F32)

    def blockdiag_in(m):
        m = m.reshape(n_half, gh, p, c)
        return jnp.einsum('hgpc,gk->hgckp', m, eye).reshape(n_half, gh * c, gh * p)

    def blockdiag_out(m):
        m = m.reshape(n_half, gh, c, p)
        return jnp.einsum('hgcp,gk->hgpkc', m, eye).reshape(n_half, gh * p, gh * c)

    bmat = jnp.concatenate([blockdiag_in(bbar_re), blockdiag_in(bbar_im)], axis=2)
    cmat = jnp.concatenate([blockdiag_out(c_re.astype(F32)), blockdiag_out(-c_im.astype(F32))], axis=1)
    a_re = jnp.broadcast_to(ar.reshape(1, g * p), (SUBLANES, g * p))
    a_im = jnp.broadcast_to(ai.reshape(1, g * p), (SUBLANES, g * p))
    return bmat.astype(BF16), cmat.astype(BF16), a_re, a_im


def kernel(x, w_in, w_out, lam_qk, subln_g, ssm_lam_re, ssm_lam_im, ssm_log_dt, ssm_b_re, ssm_b_im,
           ssm_c_re, ssm_c_im, ssm_d, glu_w, glu_b, ln1_g, ln1_b, w_ff1, w_ff2, ln2_g, ln2_b):
    bsz, seq, d_model = x.shape
    depth = w_in.shape[0]
    ssm_w = glu_w.shape[1]
    attn_w = w_out.shape[1] - ssm_w
    n_heads = attn_w // V_DIM
    qk_w = n_heads * 2 * HEAD_DIM
    assert bsz == SUBLANES, "the S5 scan keeps the batch on the sublane axis"
    assert w_in.shape[2] == 2 * qk_w + attn_w + ssm_w
    alpha = (2.0 * depth) ** 0.25
    q_scale = math.log2(math.e) / math.sqrt(HEAD_DIM)

    tl = min(512, seq)
    tq = tl
    tc = min(64, seq)

    cosq, sinq, cosk, sink = _rope_tables(seq, q_scale)
    for l in range(depth):
        lam_init = 0.8 - 0.6 * math.exp(-0.3 * l)
        w = w_in[l].astype(BF16)
        wq = w[:, :qk_w]
        wk = w[:, qk_w:2 * qk_w]
        wv = w[:, 2 * qk_w:2 * qk_w + attn_w]
        wu = w[:, 2 * qk_w + attn_w:]
        qt, k, vt, u = _in_proj(x, wq.T, wk.T, wv.T, wu, cosq, sinq, cosk, sink, tl=tl)

        attn = _attention(qt, k, vt, lam_qk[l].astype(F32), subln_g[l].astype(F32),
                          bsz=bsz, n_heads=n_heads, lam_init=lam_init, tq=tq)

        bmat, cmat, a_re, a_im = _ssm_params(ssm_lam_re[l], ssm_lam_im[l], ssm_log_dt[l],
                                             ssm_b_re[l], ssm_b_im[l], ssm_c_re[l], ssm_c_im[l])
        ssm = _ssm(u, bmat, cmat, a_re, a_im,
                   ssm_d[l].astype(F32).reshape(1, ssm_w), glu_w[l].astype(BF16),
                   glu_b[l].astype(F32).reshape(1, ssm_w), tc=tc)

        wo = w_out[l].astype(BF16)
        x = _mix_ffn(x, attn, ssm, wo[:attn_w], wo[attn_w:], ln1_g[l].astype(F32),
                     ln1_b[l].astype(F32), w_ff1[l].astype(BF16), w_ff2[l].astype(BF16),
                     ln2_g[l].astype(F32), ln2_b[l].astype(F32), alpha=alpha, tl=tl)
    return x
```

```python
import functools
import math

import jax
import jax.numpy as jnp
from jax import lax
from jax.experimental import pallas as pl
from jax.experimental.pallas import tpu as pltpu

F32 = jnp.float32
BF16 = jnp.bfloat16

HEAD_DIM = 64
V_DIM = 2 * HEAD_DIM
SSM_GROUP = 16
SSM_STATE = 64
ROPE_THETA = 10000.0
LN_EPS = 1e-5
RMS_EPS = 1e-5
LANES = 128
SUBLANES = 8
NEG_BIG = -1e30
CHUNK = 128
VMEM_LIMIT = 56 * 1024 * 1024


def _dot(a, b):
    return jnp.dot(a, b, preferred_element_type=F32)


def _layer_norm(y, g, b):
    mu = jnp.mean(y, axis=-1, keepdims=True)
    d = y - mu
    var = jnp.mean(d * d, axis=-1, keepdims=True)
    return d * lax.rsqrt(var + LN_EPS) * g + b


_NT = (((1,), (1,)), ((), ()))


def _rope_rows(h, cos, sin, out_dtype):
    half = HEAD_DIM // 2
    out = []
    for c in range(h.shape[0] // HEAD_DIM):
        x1 = h[c * HEAD_DIM:c * HEAD_DIM + half]
        x2 = h[c * HEAD_DIM + half:(c + 1) * HEAD_DIM]
        out.append((x1 * cos - x2 * sin).astype(out_dtype))
        out.append((x2 * cos + x1 * sin).astype(out_dtype))
    return out


def _in_proj_kernel(x_ref, wqt_ref, wkt_ref, wvt_ref, wu_ref, cosq_ref, sinq_ref, cosk_ref,
                    sink_ref, qt_ref, k_ref, vt_ref, u_ref):
    x = x_ref[...].astype(BF16)
    half = HEAD_DIM // 2
    hq = lax.dot_general(wqt_ref[...], x, _NT, preferred_element_type=F32)
    for i, rows in enumerate(_rope_rows(hq, cosq_ref[...], sinq_ref[...], BF16)):
        qt_ref[i * half:(i + 1) * half, :] = rows
    vt_ref[...] = lax.dot_general(wvt_ref[...], x, _NT, preferred_element_type=F32).astype(BF16)
    hk = lax.dot_general(wkt_ref[...], x, _NT, preferred_element_type=F32)
    kt = jnp.concatenate(_rope_rows(hk, cosk_ref[...], sink_ref[...], F32), axis=0)
    for j in range(k_ref.shape[0]):
        k_ref[j] = kt[j * LANES:(j + 1) * LANES].T.astype(BF16)
    u_ref[...] = _dot(x, wu_ref[...])


def _in_proj(x, wqt, wkt, wvt, wu, cosq, sinq, cosk, sink, *, tl):
    bsz, seq, d = x.shape
    qw = wqt.shape[0]
    kw = wkt.shape[0]
    vw = wvt.shape[0]
    uw = wu.shape[1]
    const = lambda b, t: (0, 0)
    table = pl.BlockSpec((HEAD_DIM // 2, tl), lambda b, t: (0, t))
    return pl.pallas_call(
        _in_proj_kernel,
        grid=(bsz, seq // tl),
        in_specs=[
            pl.BlockSpec((None, tl, d), lambda b, t: (b, t, 0)),
            pl.BlockSpec(wqt.shape, const),
            pl.BlockSpec(wkt.shape, const),
            pl.BlockSpec(wvt.shape, const),
            pl.BlockSpec(wu.shape, const),
            table, table, table, table,
        ],
        out_specs=[
            pl.BlockSpec((None, qw, tl), lambda b, t: (t, b, 0)),
            pl.BlockSpec((kw // LANES, tl, LANES), lambda b, t: (b, t, 0)),
            pl.BlockSpec((None, vw, tl), lambda b, t: (t, b, 0)),
            pl.BlockSpec((tl, uw), lambda b, t: (t, b)),
        ],
        out_shape=[
            jax.ShapeDtypeStruct((seq // tl, bsz * qw, tl), BF16),
            jax.ShapeDtypeStruct((bsz * kw // LANES, seq, LANES), BF16),
            jax.ShapeDtypeStruct((seq // tl, bsz * vw, tl), BF16),
            jax.ShapeDtypeStruct((seq, bsz * uw), F32),
        ],
        compiler_params=pltpu.CompilerParams(
            dimension_semantics=("parallel", "parallel"), vmem_limit_bytes=VMEM_LIMIT),
        name="in_proj",
    )(x, wqt, wkt, wvt, wu, cosq, sinq, cosk, sink)


def _masked_q(qt_ref, qi):
    qt = qt_ref[qi].astype(F32)
    rowid = lax.broadcasted_iota(jnp.int32, qt.shape, 0)
    return (jnp.where(rowid < HEAD_DIM, qt, 0.0).astype(BF16),
            jnp.where(rowid >= HEAD_DIM, qt, 0.0).astype(BF16))


def _scores(mp, kt, qm, k_ref, s_ref, cm_ref):
    tk, tq = s_ref.shape[1:]
    k0 = pl.multiple_of(kt * tk, tk)
    s = _dot(k_ref[pl.ds(k0, tk), :], qm[mp])
    s_ref[mp] = s
    cm_ref[mp] = jnp.max(s.reshape(tk // SUBLANES, SUBLANES, tq), axis=0)


def _attn_q_tile(qi, lq_ref, g_ref, qt_ref, k_ref, vt_ref, o_ref, m_ref, l_ref, acc_ref,
                 s_ref, cm_ref, p_ref, a_ref, *, tq, lam_init):
    qm = _masked_q(qt_ref, qi)
    qm_next = _masked_q(qt_ref, jnp.minimum(qi + 1, qt_ref.shape[0] - 1))
    m_ref[...] = jnp.full(m_ref.shape, NEG_BIG, F32)
    l_ref[...] = jnp.zeros(l_ref.shape, F32)
    acc_ref[...] = jnp.zeros(acc_ref.shape, F32)
    p_ref[...] = jnp.zeros(p_ref.shape, BF16)
    a_ref[...] = jnp.ones(a_ref.shape, F32)

    def softmax_tile(mp, masked):
        if masked:
            cm = jnp.full((SUBLANES, tq), NEG_BIG, F32)
            for c in range(tq // CHUNK):
                sc = s_ref[mp, c * CHUNK:(c + 1) * CHUNK, :]
                row = lax.broadcasted_iota(jnp.int32, sc.shape, 0) + c * CHUNK
                col = lax.broadcasted_iota(jnp.int32, sc.shape, 1)
                sc = jnp.where(row <= col, sc, NEG_BIG)
                s_ref[mp, c * CHUNK:(c + 1) * CHUNK, :] = sc
                cm = jnp.maximum(cm, jnp.max(sc.reshape(CHUNK // SUBLANES, SUBLANES, tq), axis=0))
        else:
            cm = cm_ref[mp]
        m_prev = m_ref[mp]
        m_new = jnp.maximum(m_prev, jnp.max(cm, axis=0, keepdims=True))
        alpha = jnp.exp2(m_prev - m_new)
        m_ref[mp] = m_new
        lsum = jnp.zeros((SUBLANES, tq), F32)
        for c in range(tq // CHUNK):
            pc = jnp.exp2(s_ref[mp, c * CHUNK:(c + 1) * CHUNK, :] - m_new)
            lsum = lsum + jnp.sum(pc.reshape(CHUNK // SUBLANES, SUBLANES, tq), axis=0)
            p_ref[mp, c * CHUNK:(c + 1) * CHUNK, :] = pc.astype(BF16)
        l_ref[mp] = alpha * l_ref[mp] + lsum
        return alpha

    def pv(kt):
        vt = vt_ref[kt]
        for mp in range(2):
            acc_ref[mp] = a_ref[mp] * acc_ref[mp] + _dot(vt, p_ref[mp])

    def step(kt, masked):
        pv(jnp.maximum(kt - 1, 0))
        for mp in range(2):
            a_ref[mp] = softmax_tile(mp, masked)
        for mp in range(2):
            if masked:
                _scores(mp, 0, qm_next, k_ref, s_ref, cm_ref)
            else:
                _scores(mp, kt + 1, qm, k_ref, s_ref, cm_ref)

    def body(i, carry):
        step(2 * i, False)
        step(2 * i + 1, False)
        return carry

    lax.fori_loop(0, qi // 2, body, 0)

    @pl.when(qi % 2 == 1)
    def _():
        step(qi - 1, False)

    step(qi, True)
    pv(qi)

    lq = lq_ref[...]
    lam = (jnp.exp(jnp.sum(lq[0:1] * lq[1:2], axis=1, keepdims=True))
           - jnp.exp(jnp.sum(lq[2:3] * lq[3:4], axis=1, keepdims=True)) + lam_init)
    l0 = jnp.sum(l_ref[0], axis=0, keepdims=True)
    l1 = jnp.sum(l_ref[1], axis=0, keepdims=True)
    ot = acc_ref[0] / l0 - lam * (acc_ref[1] / l1)
    ms = jnp.mean(ot * ot, axis=0, keepdims=True)
    ot = ot * lax.rsqrt(ms + RMS_EPS) * g_ref[...] * (1.0 - lam_init)
    o_ref[pl.ds(pl.multiple_of(qi * tq, tq), tq), :] = ot.T.astype(o_ref.dtype)


def _attn_kernel(*refs, tq, lam_init):
    qt_ref, k_ref = refs[2], refs[3]
    s_ref, cm_ref = refs[9], refs[10]
    n_q = qt_ref.shape[0]
    qm0 = _masked_q(qt_ref, 0)
    for mp in range(2):
        _scores(mp, 0, qm0, k_ref, s_ref, cm_ref)

    def q_tile(qi, carry):
        _attn_q_tile(qi, *refs, tq=tq, lam_init=lam_init)
        return carry

    lax.fori_loop(0, n_q, q_tile, 0)


def _attention(qt, k, vt, lam_qk, subln_g, *, bsz, n_heads, lam_init, tq):
    seq = k.shape[1]
    return pl.pallas_call(
        functools.partial(_attn_kernel, tq=tq, lam_init=lam_init),
        grid=(bsz, n_heads),
        in_specs=[
            pl.BlockSpec(lam_qk.shape, lambda b, h: (0, 0)),
            pl.BlockSpec((V_DIM, 1), lambda b, h: (0, 0)),
            pl.BlockSpec((seq // tq, V_DIM, tq), lambda b, h: (0, b * n_heads + h, 0)),
            pl.BlockSpec((None, seq, V_DIM), lambda b, h: (b * n_heads + h, 0, 0)),
            pl.BlockSpec((seq // tq, V_DIM, tq), lambda b, h: (0, b * n_heads + h, 0)),
        ],
        out_specs=pl.BlockSpec((None, seq, V_DIM), lambda b, h: (b * n_heads + h, 0, 0)),
        out_shape=jax.ShapeDtypeStruct((bsz * n_heads, seq, V_DIM), BF16),
        scratch_shapes=[
            pltpu.VMEM((2, 1, tq), F32),
            pltpu.VMEM((2, SUBLANES, tq), F32),
            pltpu.VMEM((2, V_DIM, tq), F32),
            pltpu.VMEM((2, tq, tq), F32),
            pltpu.VMEM((2, SUBLANES, tq), F32),
            pltpu.VMEM((2, tq, tq), BF16),
            pltpu.VMEM((2, 1, tq), F32),
        ],
        compiler_params=pltpu.CompilerParams(
            dimension_semantics=("parallel", "parallel"), vmem_limit_bytes=VMEM_LIMIT),
        name="diff_attn",
    )(lam_qk, subln_g.reshape(V_DIM, 1), qt, k, vt)


def _gelu_tanh(y):
    c = math.sqrt(2.0 / math.pi)
    return 0.5 * y * (1.0 + jnp.tanh(c * (y + 0.044715 * (y * y * y))))


def _anchor(x):
    bits = pltpu.bitcast(x, jnp.uint32)
    return pltpu.bitcast(lax.shift_right_logical(bits, jnp.uint32(32)), F32)


def _ssm_kernel(u_ref, un_ref, bmat_ref, cmat_ref, ar_ref, ai_ref, d_ref, gw_ref, gb_ref,
                out_ref, urow_ref, ubf_ref, bu0_ref, bu1_ref, sb_ref, st_ref, *, tc, slab):
    n_half = bmat_ref.shape[0]
    uk = bmat_ref.shape[1]
    hw = bmat_ref.shape[2]
    cw = hw // 2
    step = pl.program_id(0)

    n_col = 2 * LANES
    bu = (bu0_ref, bu1_ref)

    def relayout(src_ref, slot):
        u = pltpu.einshape("t(bc)->(tb)c", src_ref[...], b=SUBLANES)
        urow_ref[slot] = u
        ubf_ref[slot] = u.astype(BF16)

    def project_tile(slot, j):
        hf, c = divmod(j * n_col, hw)
        bu[slot][:, hf * hw + c:hf * hw + c + n_col] = _dot(
            ubf_ref[slot, :, hf * uk:(hf + 1) * uk], bmat_ref[hf, :, c:c + n_col])

    n_tiles = n_half * hw // n_col

    @pl.when(step == 0)
    def _():
        st_ref[...] = jnp.zeros(st_ref.shape, F32)
        relayout(u_ref, 0)
        for j in range(n_tiles):
            project_tile(0, j)

    def chunk(cur, nxt):
        relayout(un_ref, nxt)
        slabs = [(hf * hw + sl * slab, hf * hw + cw + sl * slab, hf * cw + sl * slab)
                 for hf in range(n_half) for sl in range(cw // slab)]
        state = [(st_ref[:, cr:cr + slab], st_ref[:, ci:ci + slab]) for cr, ci, _ in slabs]
        n_pairs = tc // 2
        tile_at = {(j * n_pairs) // n_tiles: j for j in range(n_tiles)} if n_pairs >= n_tiles else {}
        for t2 in range(n_pairs):
            j = tile_at.get(t2)
            if j is not None:
                hf_j, c_j = divmod(j * n_col, hw)
                if j > 0:
                    z = _anchor(state[0][0][:, :LANES])
                    head = (slice(0, 2 * SUBLANES), slice(hf_j * uk, hf_j * uk + LANES))
                    ubf_ref[(nxt,) + head] = ubf_ref[(nxt,) + head] + jnp.concatenate([z, z], 0).astype(BF16)
                project_tile(nxt, j)
                if j > 0:
                    hf_p, c_p = divmod((j - 1) * n_col, hw)
                    z = _anchor(bu[nxt][0:SUBLANES, hf_p * hw + c_p:hf_p * hw + c_p + LANES])
                    zz = jnp.concatenate([z] * (slab // LANES), axis=1)
                    state = [(s_re + zz, s_im) for s_re, s_im in state]
            rows2 = slice(2 * t2 * SUBLANES, (2 * t2 + 2) * SUBLANES)
            for i, (cr, ci, ca) in enumerate(slabs):
                a_re = ar_ref[:, ca:ca + slab]
                a_im = ai_ref[:, ca:ca + slab]
                s_re, s_im = state[i]
                pair_re, pair_im = [], []
                for t in (2 * t2, 2 * t2 + 1):
                    rows = slice(t * SUBLANES, (t + 1) * SUBLANES)
                    n_re = a_re * s_re - a_im * s_im + bu[cur][rows, cr:cr + slab]
                    n_im = a_re * s_im + a_im * s_re + bu[cur][rows, ci:ci + slab]
                    s_re, s_im = n_re, n_im
                    pair_re.append(n_re)
                    pair_im.append(n_im)
                state[i] = (s_re, s_im)
                sb_ref[rows2, cr:cr + slab] = jnp.concatenate(pair_re, axis=0).astype(BF16)
                sb_ref[rows2, ci:ci + slab] = jnp.concatenate(pair_im, axis=0).astype(BF16)
        for (cr, ci, _), (s_re, s_im) in zip(slabs, state):
            st_ref[:, cr:cr + slab] = s_re
            st_ref[:, ci:ci + slab] = s_im
        if not tile_at:
            for j in range(n_tiles):
                project_tile(nxt, j)

        ys = [_dot(sb_ref[:, hf * hw:(hf + 1) * hw], cmat_ref[hf]) for hf in range(n_half)]
        y = jnp.concatenate(ys, axis=1) + d_ref[...] * urow_ref[cur]
        y = _gelu_tanh(y)
        z = _dot(y.astype(BF16), gw_ref[...]) + gb_ref[...]
        out = pltpu.einshape("(tb)c->t(bc)", y * jax.nn.sigmoid(z), b=SUBLANES)
        out_ref[...] = out.astype(out_ref.dtype)

    @pl.when(step % 2 == 0)
    def _():
        chunk(0, 1)

    @pl.when(step % 2 == 1)
    def _():
        chunk(1, 0)


def _ssm(u, bmat, cmat, a_re, a_im, d_skip, glu_w, glu_b, *, tc, slab=512):
    seq, bw = u.shape
    blk = tc * SUBLANES
    state_w = bmat.shape[0] * bmat.shape[2]
    n_chunks = seq // tc
    c2 = lambda t: (0, 0)
    c3 = lambda t: (0, 0, 0)
    return pl.pallas_call(
        functools.partial(_ssm_kernel, tc=tc, slab=slab),
        grid=(n_chunks,),
        in_specs=[
            pl.BlockSpec((tc, bw), lambda t: (t, 0)),
            pl.BlockSpec((tc, bw), lambda t: (jnp.minimum(t + 1, n_chunks - 1), 0)),
            pl.BlockSpec(bmat.shape, c3),
            pl.BlockSpec(cmat.shape, c3),
            pl.BlockSpec(a_re.shape, c2),
            pl.BlockSpec(a_im.shape, c2),
            pl.BlockSpec(d_skip.shape, c2),
            pl.BlockSpec(glu_w.shape, c2),
            pl.BlockSpec(glu_b.shape, c2),
        ],
        out_specs=pl.BlockSpec((tc, bw), lambda t: (t, 0)),
        out_shape=jax.ShapeDtypeStruct((seq, bw), BF16),
        scratch_shapes=[
            pltpu.VMEM((2, blk, bw // SUBLANES), F32),
            pltpu.VMEM((2, blk, bw // SUBLANES), BF16),
            pltpu.VMEM((blk, state_w), F32),
            pltpu.VMEM((blk, state_w), F32),
            pltpu.VMEM((blk, state_w), BF16),
            pltpu.VMEM((SUBLANES, state_w), F32),
        ],
        compiler_params=pltpu.CompilerParams(
            dimension_semantics=("arbitrary",), vmem_limit_bytes=VMEM_LIMIT),
        name="s5_ssm",
    )(u, u, bmat, cmat, a_re, a_im, d_skip, glu_w, glu_b)


def _mix_ffn_kernel(x_ref, a_ref, s_ref, wa_ref, ws_ref, g1_ref, b1_ref, w1_ref, w2_ref,
                    g2_ref, b2_ref, o_ref, *, alpha):
    a = jnp.concatenate([a_ref[h] for h in range(a_ref.shape[0])], axis=1)
    mix = _dot(a, wa_ref[...]) + _dot(s_ref[...], ws_ref[...])
    x1 = _layer_norm(alpha * x_ref[...] + mix, g1_ref[...], b1_ref[...])
    h = jnp.maximum(_dot(x1.astype(BF16), w1_ref[...]), 0.0)
    ff = _dot((h * h).astype(BF16), w2_ref[...])
    o_ref[...] = _layer_norm(alpha * x1 + ff, g2_ref[...], b2_ref[...])


def _mix_ffn(x, attn, ssm, wa, ws, g1, b1, w1, w2, g2, b2, *, alpha, tl):
    bsz, seq, d = x.shape
    aw = wa.shape[0]
    sw = ws.shape[0]
    const = lambda bb, t: (0, 0)
    resident = lambda arr: pl.BlockSpec(arr.shape, const, pipeline_mode=pl.Buffered(1))
    row = lambda v: v.reshape(1, d)
    return pl.pallas_call(
        functools.partial(_mix_ffn_kernel, alpha=alpha),
        grid=(bsz, seq // tl),
        in_specs=[
            pl.BlockSpec((None, tl, d), lambda bb, t: (bb, t, 0)),
            pl.BlockSpec((aw // V_DIM, tl, V_DIM), lambda bb, t: (bb, t, 0)),
            pl.BlockSpec((tl, sw), lambda bb, t: (t, bb)),
            resident(wa),
            resident(ws),
            pl.BlockSpec((1, d), const),
            pl.BlockSpec((1, d), const),
            resident(w1),
            resident(w2),
            pl.BlockSpec((1, d), const),
            pl.BlockSpec((1, d), const),
        ],
        out_specs=pl.BlockSpec((None, tl, d), lambda bb, t: (bb, t, 0)),
        out_shape=jax.ShapeDtypeStruct((bsz, seq, d), F32),
        compiler_params=pltpu.CompilerParams(
            dimension_semantics=("parallel", "parallel"), vmem_limit_bytes=VMEM_LIMIT),
        name="mix_ffn_ln",
    )(x, attn, ssm, wa, ws, row(g1), row(b1), w1, w2, row(g2), row(b2))


def _rope_tables(seq, q_scale):
    pos = jnp.arange(seq, dtype=F32)
    inv_freq = ROPE_THETA ** (-jnp.arange(0, HEAD_DIM, 2, dtype=F32) / HEAD_DIM)
    ang = inv_freq[:, None] * pos[None, :]
    cos = jnp.cos(ang)
    sin = jnp.sin(ang)
    return cos * q_scale, sin * q_scale, cos, sin


def _ssm_params(lam_re, lam_im, log_dt, b_re, b_im, c_re, c_im, n_half=2):
    g, p = lam_re.shape
    c = b_re.shape[-1]
    lr = lam_re.astype(F32)
    li = lam_im.astype(F32)
    dt = jnp.exp(log_dt.astype(F32))[:, None]
    mag = jnp.exp(lr * dt)
    ar = mag * jnp.cos(li * dt)
    ai = mag * jnp.sin(li * dt)
    den = lr * lr + li * li
    fr = ((ar - 1.0) * lr + ai * li) / den
    fi = (ai * lr - (ar - 1.0) * li) / den
    br = b_re.astype(F32)
    bi = b_im.astype(F32)
    bbar_re = fr[..., None] * br - fi[..., None] * bi
    bbar_im = fr[..., None] * bi + fi[..., None] * br
    gh = g // n_half
    eye = jnp.eye(gh, dtype=F32)

    def blockdiag_in(m):
        m = m.reshape(n_half, gh, p, c)
        return jnp.einsum('hgpc,gk->hgckp', m, eye).reshape(n_half, gh * c, gh * p)

    def blockdiag_out(m):
        m = m.reshape(n_half, gh, c, p)
        return jnp.einsum('hgcp,gk->hgpkc', m, eye).reshape(n_half, gh * p, gh * c)

    bmat = jnp.concatenate([blockdiag_in(bbar_re), blockdiag_in(bbar_im)], axis=2)
    cmat = jnp.concatenate([blockdiag_out(c_re.astype(F32)), blockdiag_out(-c_im.astype(F32))], axis=1)
    a_re = jnp.broadcast_to(ar.reshape(1, g * p), (SUBLANES, g * p))
    a_im = jnp.broadcast_to(ai.reshape(1, g * p), (SUBLANES, g * p))
    return bmat.astype(BF16), cmat.astype(BF16), a_re, a_im


def kernel(x, w_in, w_out, lam_qk, subln_g, ssm_lam_re, ssm_lam_im, ssm_log_dt, ssm_b_re, ssm_b_im,
           ssm_c_re, ssm_c_im, ssm_d, glu_w, glu_b, ln1_g, ln1_b, w_ff1, w_ff2, ln2_g, ln2_b):
    bsz, seq, d_model = x.shape
    depth = w_in.shape[0]
    ssm_w = glu_w.shape[1]
    attn_w = w_out.shape[1] - ssm_w
    n_heads = attn_w // V_DIM
    qk_w = n_heads * 2 * HEAD_DIM
    assert bsz == SUBLANES, "the S5 scan keeps the batch on the sublane axis"
    assert w_in.shape[2] == 2 * qk_w + attn_w + ssm_w
    alpha = (2.0 * depth) ** 0.25
    q_scale = math.log2(math.e) / math.sqrt(HEAD_DIM)

    tl = min(512, seq)
    tq = tl
    tc = min(64, seq)

    cosq, sinq, cosk, sink = _rope_tables(seq, q_scale)
    for l in range(depth):
        lam_init = 0.8 - 0.6 * math.exp(-0.3 * l)
        w = w_in[l].astype(BF16)
        wq = w[:, :qk_w]
        wk = w[:, qk_w:2 * qk_w]
        wv = w[:, 2 * qk_w:2 * qk_w + attn_w]
        wu = w[:, 2 * qk_w + attn_w:]
        qt, k, vt, u = _in_proj(x, wq.T, wk.T, wv.T, wu, cosq, sinq, cosk, sink, tl=tl)

        attn = _attention(qt, k, vt, lam_qk[l].astype(F32), subln_g[l].astype(F32),
                          bsz=bsz, n_heads=n_heads, lam_init=lam_init, tq=tq)

        bmat, cmat, a_re, a_im = _ssm_params(ssm_lam_re[l], ssm_lam_im[l], ssm_log_dt[l],
                                             ssm_b_re[l], ssm_b_im[l], ssm_c_re[l], ssm_c_im[l])
        ssm = _ssm(u, bmat, cmat, a_re, a_im,
                   ssm_d[l].astype(F32).reshape(1, ssm_w), glu_w[l].astype(BF16),
                   glu_b[l].astype(F32).reshape(1, ssm_w), tc=tc)

        wo = w_out[l].astype(BF16)
        x = _mix_ffn(x, attn, ssm, wo[:attn_w], wo[attn_w:], ln1_g[l].astype(F32),
                     ln1_b[l].astype(F32), w_ff1[l].astype(BF16), w_ff2[l].astype(BF16),
                     ln2_g[l].astype(F32), ln2_b[l].astype(F32), alpha=alpha, tl=tl)
    return x
```

```python
import functools
import math

import jax
import jax.numpy as jnp
from jax import lax
from jax.experimental import pallas as pl
from jax.experimental.pallas import tpu as pltpu

F32 = jnp.float32
BF16 = jnp.bfloat16

HEAD_DIM = 64
V_DIM = 2 * HEAD_DIM
SSM_GROUP = 16
SSM_STATE = 64
ROPE_THETA = 10000.0
LN_EPS = 1e-5
RMS_EPS = 1e-5
LANES = 128
SUBLANES = 8
NEG_BIG = -1e30
CHUNK = 128
VMEM_LIMIT = 56 * 1024 * 1024


def _dot(a, b):
    return jnp.dot(a, b, preferred_element_type=F32)


def _layer_norm(y, g, b):
    mu = jnp.mean(y, axis=-1, keepdims=True)
    d = y - mu
    var = jnp.mean(d * d, axis=-1, keepdims=True)
    return d * lax.rsqrt(var + LN_EPS) * g + b


_NT = (((1,), (1,)), ((), ()))


def _rope_rows(h, cos, sin, out_dtype):
    half = HEAD_DIM // 2
    out = []
    for c in range(h.shape[0] // HEAD_DIM):
        x1 = h[c * HEAD_DIM:c * HEAD_DIM + half]
        x2 = h[c * HEAD_DIM + half:(c + 1) * HEAD_DIM]
        out.append((x1 * cos - x2 * sin).astype(out_dtype))
        out.append((x2 * cos + x1 * sin).astype(out_dtype))
    return out


def _in_proj_kernel(x_ref, wqt_ref, wkt_ref, wvt_ref, wu_ref, cosq_ref, sinq_ref, cosk_ref,
                    sink_ref, qt_ref, k_ref, vt_ref, u_ref):
    x = x_ref[...].astype(BF16)
    half = HEAD_DIM // 2
    hq = lax.dot_general(wqt_ref[...], x, _NT, preferred_element_type=F32)
    for i, rows in enumerate(_rope_rows(hq, cosq_ref[...], sinq_ref[...], BF16)):
        qt_ref[i * half:(i + 1) * half, :] = rows
    vt_ref[...] = lax.dot_general(wvt_ref[...], x, _NT, preferred_element_type=F32).astype(BF16)
    hk = lax.dot_general(wkt_ref[...], x, _NT, preferred_element_type=F32)
    kt = jnp.concatenate(_rope_rows(hk, cosk_ref[...], sink_ref[...], F32), axis=0)
    for j in range(k_ref.shape[0]):
        k_ref[j] = kt[j * LANES:(j + 1) * LANES].T.astype(BF16)
    u_ref[...] = _dot(x, wu_ref[...])


def _in_proj(x, wqt, wkt, wvt, wu, cosq, sinq, cosk, sink, *, tl):
    bsz, seq, d = x.shape
    qw = wqt.shape[0]
    kw = wkt.shape[0]
    vw = wvt.shape[0]
    uw = wu.shape[1]
    const = lambda b, t: (0, 0)
    table = pl.BlockSpec((HEAD_DIM // 2, tl), lambda b, t: (0, t))
    return pl.pallas_call(
        _in_proj_kernel,
        grid=(bsz, seq // tl),
        in_specs=[
            pl.BlockSpec((None, tl, d), lambda b, t: (b, t, 0)),
            pl.BlockSpec(wqt.shape, const),
            pl.BlockSpec(wkt.shape, const),
            pl.BlockSpec(wvt.shape, const),
            pl.BlockSpec(wu.shape, const),
            table, table, table, table,
        ],
        out_specs=[
            pl.BlockSpec((None, qw, tl), lambda b, t: (t, b, 0)),
            pl.BlockSpec((kw // LANES, tl, LANES), lambda b, t: (b, t, 0)),
            pl.BlockSpec((None, vw, tl), lambda b, t: (t, b, 0)),
            pl.BlockSpec((tl, uw), lambda b, t: (t, b)),
        ],
        out_shape=[
            jax.ShapeDtypeStruct((seq // tl, bsz * qw, tl), BF16),
            jax.ShapeDtypeStruct((bsz * kw // LANES, seq, LANES), BF16),
            jax.ShapeDtypeStruct((seq // tl, bsz * vw, tl), BF16),
            jax.ShapeDtypeStruct((seq, bsz * uw), F32),
        ],
        compiler_params=pltpu.CompilerParams(
            dimension_semantics=("parallel", "parallel"), vmem_limit_bytes=VMEM_LIMIT),
        name="in_proj",
    )(x, wqt, wkt, wvt, wu, cosq, sinq, cosk, sink)


def _masked_q(qt_ref, qi):
    qt = qt_ref[qi].astype(F32)
    rowid = lax.broadcasted_iota(jnp.int32, qt.shape, 0)
    return (jnp.where(rowid < HEAD_DIM, qt, 0.0).astype(BF16),
            jnp.where(rowid >= HEAD_DIM, qt, 0.0).astype(BF16))


def _scores(mp, kt, qm, k_ref, s_ref, cm_ref):
    tk, tq = s_ref.shape[1:]
    k0 = pl.multiple_of(kt * tk, tk)
    s = _dot(k_ref[pl.ds(k0, tk), :], qm[mp])
    s_ref[mp] = s
    cm_ref[mp] = jnp.max(s.reshape(tk // SUBLANES, SUBLANES, tq), axis=0)


def _attn_kernel(lq_ref, g_ref, qt_ref, k_ref, vt_ref, o_ref, m_ref, l_ref, acc_ref,
                 s_ref, cm_ref, p_ref, a_ref, *, tq, lam_init):
    n_q = qt_ref.shape[0]
    last = n_q - 1

    def scores(q_idx, kt):
        qm = _masked_q(qt_ref, q_idx)
        for mp in range(2):
            _scores(mp, kt, qm, k_ref, s_ref, cm_ref)

    def softmax_tile(mp, masked):
        if masked:
            cm = jnp.full((SUBLANES, tq), NEG_BIG, F32)
            for c in range(tq // CHUNK):
                sc = s_ref[mp, c * CHUNK:(c + 1) * CHUNK, :]
                row = lax.broadcasted_iota(jnp.int32, sc.shape, 0) + c * CHUNK
                col = lax.broadcasted_iota(jnp.int32, sc.shape, 1)
                sc = jnp.where(row <= col, sc, NEG_BIG)
                s_ref[mp, c * CHUNK:(c + 1) * CHUNK, :] = sc
                cm = jnp.maximum(cm, jnp.max(sc.reshape(CHUNK // SUBLANES, SUBLANES, tq), axis=0))
        else:
            cm = cm_ref[mp]
        m_prev = m_ref[mp]
        m_new = jnp.maximum(m_prev, jnp.max(cm, axis=0, keepdims=True))
        alpha = jnp.exp2(m_prev - m_new)
        m_ref[mp] = m_new
        lsum = jnp.zeros((SUBLANES, tq), F32)
        for c in range(tq // CHUNK):
            pc = jnp.exp2(s_ref[mp, c * CHUNK:(c + 1) * CHUNK, :] - m_new)
            lsum = lsum + jnp.sum(pc.reshape(CHUNK // SUBLANES, SUBLANES, tq), axis=0)
            p_ref[mp, c * CHUNK:(c + 1) * CHUNK, :] = pc.astype(BF16)
        l_ref[mp] = alpha * l_ref[mp] + lsum
        a_ref[mp] = alpha

    def pv(kt):
        vt = vt_ref[kt]
        for mp in range(2):
            acc_ref[mp] = a_ref[mp] * acc_ref[mp] + _dot(vt, p_ref[mp])

    def finalize(qi):
        lq = lq_ref[...]
        lam = (jnp.exp(jnp.sum(lq[0:1] * lq[1:2], axis=1, keepdims=True))
               - jnp.exp(jnp.sum(lq[2:3] * lq[3:4], axis=1, keepdims=True)) + lam_init)
        l0 = jnp.sum(l_ref[0], axis=0, keepdims=True)
        l1 = jnp.sum(l_ref[1], axis=0, keepdims=True)
        ot = acc_ref[0] / l0 - lam * (acc_ref[1] / l1)
        ms = jnp.mean(ot * ot, axis=0, keepdims=True)
        ot = ot * lax.rsqrt(ms + RMS_EPS) * g_ref[...] * (1.0 - lam_init)
        o_ref[pl.ds(pl.multiple_of(qi * tq, tq), tq), :] = ot.T.astype(o_ref.dtype)

    def last_tile_of(qi):
        return jnp.where(qi >= 1, qi - 1, qi)

    def prefetch_after(qi, kt_next):
        same = kt_next < qi
        nq = jnp.minimum(qi + 1, last)
        scores(jnp.where(same, qi, nq), jnp.where(same, kt_next, nq))

    def step(qi, kt):
        pv(jnp.where(kt == 0, qi, kt - 1))
        for mp in range(2):
            softmax_tile(mp, False)
        prefetch_after(qi, kt + 1)

    def q_tile(qi, carry):
        prev = jnp.maximum(qi - 1, 0)
        pv(last_tile_of(prev))
        finalize(prev)
        m_ref[...] = jnp.full(m_ref.shape, NEG_BIG, F32)
        l_ref[...] = jnp.zeros(l_ref.shape, F32)
        acc_ref[...] = jnp.zeros(acc_ref.shape, F32)
        for mp in range(2):
            softmax_tile(mp, True)
        prefetch_after(qi, 0)

        def body(i, c):
            step(qi, 2 * i)
            step(qi, 2 * i + 1)
            return c

        lax.fori_loop(0, qi // 2, body, 0)

        @pl.when(qi % 2 == 1)
        def _():
            step(qi, qi - 1)

        return carry

    p_ref[...] = jnp.zeros(p_ref.shape, BF16)
    a_ref[...] = jnp.ones(a_ref.shape, F32)
    acc_ref[...] = jnp.zeros(acc_ref.shape, F32)
    l_ref[...] = jnp.ones(l_ref.shape, F32)
    scores(0, 0)
    lax.fori_loop(0, n_q, q_tile, 0)
    pv(last_tile_of(last))
    finalize(last)


def _attention(qt, k, vt, lam_qk, subln_g, *, bsz, n_heads, lam_init, tq):
    seq = k.shape[1]
    return pl.pallas_call(
        functools.partial(_attn_kernel, tq=tq, lam_init=lam_init),
        grid=(bsz, n_heads),
        in_specs=[
            pl.BlockSpec(lam_qk.shape, lambda b, h: (0, 0)),
            pl.BlockSpec((V_DIM, 1), lambda b, h: (0, 0)),
            pl.BlockSpec((seq // tq, V_DIM, tq), lambda b, h: (0, b * n_heads + h, 0)),
            pl.BlockSpec((None, seq, V_DIM), lambda b, h: (b * n_heads + h, 0, 0)),
            pl.BlockSpec((seq // tq, V_DIM, tq), lambda b, h: (0, b * n_heads + h, 0)),
        ],
        out_specs=pl.BlockSpec((None, seq, V_DIM), lambda b, h: (b * n_heads + h, 0, 0)),
        out_shape=jax.ShapeDtypeStruct((bsz * n_heads, seq, V_DIM), BF16),
        scratch_shapes=[
            pltpu.VMEM((2, 1, tq), F32),
            pltpu.VMEM((2, SUBLANES, tq), F32),
            pltpu.VMEM((2, V_DIM, tq), F32),
            pltpu.VMEM((2, tq, tq), F32),
            pltpu.VMEM((2, SUBLANES, tq), F32),
            pltpu.VMEM((2, tq, tq), BF16),
            pltpu.VMEM((2, 1, tq), F32),
        ],
        compiler_params=pltpu.CompilerParams(
            dimension_semantics=("parallel", "parallel"), vmem_limit_bytes=VMEM_LIMIT),
        name="diff_attn",
    )(lam_qk, subln_g.reshape(V_DIM, 1), qt, k, vt)


def _gelu_tanh(y):
    c = math.sqrt(2.0 / math.pi)
    return 0.5 * y * (1.0 + jnp.tanh(c * (y + 0.044715 * (y * y * y))))


def _anchor(x):
    bits = pltpu.bitcast(x, jnp.uint32)
    return pltpu.bitcast(lax.shift_right_logical(bits, jnp.uint32(32)), F32)


def _ssm_kernel(u_ref, un_ref, bmat_ref, cmat_ref, ar_ref, ai_ref, d_ref, gw_ref, gb_ref,
                out_ref, urow_ref, ubf_ref, bu0_ref, bu1_ref, sb_ref, st_ref, *, tc, slab):
    n_half = bmat_ref.shape[0]
    uk = bmat_ref.shape[1]
    hw = bmat_ref.shape[2]
    cw = hw // 2
    step = pl.program_id(0)

    n_col = 2 * LANES
    bu = (bu0_ref, bu1_ref)

    def relayout(src_ref, slot):
        u = pltpu.einshape("t(bc)->(tb)c", src_ref[...], b=SUBLANES)
        urow_ref[slot] = u
        ubf_ref[slot] = u.astype(BF16)

    def project_tile(slot, j):
        hf, c = divmod(j * n_col, hw)
        bu[slot][:, hf * hw + c:hf * hw + c + n_col] = _dot(
            ubf_ref[slot, :, hf * uk:(hf + 1) * uk], bmat_ref[hf, :, c:c + n_col])

    n_tiles = n_half * hw // n_col

    @pl.when(step == 0)
    def _():
        st_ref[...] = jnp.zeros(st_ref.shape, F32)
        relayout(u_ref, 0)
        for j in range(n_tiles):
            project_tile(0, j)

    def chunk(cur, nxt):
        relayout(un_ref, nxt)
        slabs = [(hf * hw + sl * slab, hf * hw + cw + sl * slab, hf * cw + sl * slab)
                 for hf in range(n_half) for sl in range(cw // slab)]
        state = [(st_ref[:, cr:cr + slab], st_ref[:, ci:ci + slab]) for cr, ci, _ in slabs]
        n_pairs = tc // 2
        tile_at = {(j * n_pairs) // n_tiles: j for j in range(n_tiles)} if n_pairs >= n_tiles else {}
        for t2 in range(n_pairs):
            j = tile_at.get(t2)
            if j is not None:
                hf_j, c_j = divmod(j * n_col, hw)
                if j > 0:
                    z = _anchor(state[0][0][:, :LANES])
                    head = (slice(0, 2 * SUBLANES), slice(hf_j * uk, hf_j * uk + LANES))
                    ubf_ref[(nxt,) + head] = ubf_ref[(nxt,) + head] + jnp.concatenate([z, z], 0).astype(BF16)
                project_tile(nxt, j)
                if j > 0:
                    hf_p, c_p = divmod((j - 1) * n_col, hw)
                    z = _anchor(bu[nxt][0:SUBLANES, hf_p * hw + c_p:hf_p * hw + c_p + LANES])
                    zz = jnp.concatenate([z] * (slab // LANES), axis=1)
                    state = [(s_re + zz, s_im) for s_re, s_im in state]
            rows2 = slice(2 * t2 * SUBLANES, (2 * t2 + 2) * SUBLANES)
            for i, (cr, ci, ca) in enumerate(slabs):
                a_re = ar_ref[:, ca:ca + slab]
                a_im = ai_ref[:, ca:ca + slab]
                s_re, s_im = state[i]
                pair_re, pair_im = [], []
                for t in (2 * t2, 2 * t2 + 1):
                    rows = slice(t * SUBLANES, (t + 1) * SUBLANES)
                    n_re = a_re * s_re - a_im * s_im + bu[cur][rows, cr:cr + slab]
                    n_im = a_re * s_im + a_im * s_re + bu[cur][rows, ci:ci + slab]
                    s_re, s_im = n_re, n_im
                    pair_re.append(n_re)
                    pair_im.append(n_im)
                state[i] = (s_re, s_im)
                sb_ref[rows2, cr:cr + slab] = jnp.concatenate(pair_re, axis=0).astype(BF16)
                sb_ref[rows2, ci:ci + slab] = jnp.concatenate(pair_im, axis=0).astype(BF16)
        for (cr, ci, _), (s_re, s_im) in zip(slabs, state):
            st_ref[:, cr:cr + slab] = s_re
            st_ref[:, ci:ci + slab] = s_im
        if not tile_at:
            for j in range(n_tiles):
                project_tile(nxt, j)

        ys = [_dot(sb_ref[:, hf * hw:(hf + 1) * hw], cmat_ref[hf]) for hf in range(n_half)]
        y = jnp.concatenate(ys, axis=1) + d_ref[...] * urow_ref[cur]
        y = _gelu_tanh(y)
        z = _dot(y.astype(BF16), gw_ref[...]) + gb_ref[...]
        out = pltpu.einshape("(tb)c->t(bc)", y * jax.nn.sigmoid(z), b=SUBLANES)
        out_ref[...] = out.astype(out_ref.dtype)

    @pl.when(step % 2 == 0)
    def _():
        chunk(0, 1)

    @pl.when(step % 2 == 1)
    def _():
        chunk(1, 0)


def _ssm(u, bmat, cmat, a_re, a_im, d_skip, glu_w, glu_b, *, tc, slab=512):
    seq, bw = u.shape
    blk = tc * SUBLANES
    state_w = bmat.shape[0] * bmat.shape[2]
    n_chunks = seq // tc
    c2 = lambda t: (0, 0)
    c3 = lambda t: (0, 0, 0)
    return pl.pallas_call(
        functools.partial(_ssm_kernel, tc=tc, slab=slab),
        grid=(n_chunks,),
        in_specs=[
            pl.BlockSpec((tc, bw), lambda t: (t, 0)),
            pl.BlockSpec((tc, bw), lambda t: (jnp.minimum(t + 1, n_chunks - 1), 0)),
            pl.BlockSpec(bmat.shape, c3),
            pl.BlockSpec(cmat.shape, c3),
            pl.BlockSpec(a_re.shape, c2),
            pl.BlockSpec(a_im.shape, c2),
            pl.BlockSpec(d_skip.shape, c2),
            pl.BlockSpec(glu_w.shape, c2),
            pl.BlockSpec(glu_b.shape, c2),
        ],
        out_specs=pl.BlockSpec((tc, bw), lambda t: (t, 0)),
        out_shape=jax.ShapeDtypeStruct((seq, bw), BF16),
        scratch_shapes=[
            pltpu.VMEM((2, blk, bw // SUBLANES), F32),
            pltpu.VMEM((2, blk, bw // SUBLANES), BF16),
            pltpu.VMEM((blk, state_w), F32),
            pltpu.VMEM((blk, state_w), F32),
            pltpu.VMEM((blk, state_w), BF16),
            pltpu.VMEM((SUBLANES, state_w), F32),
        ],
        compiler_params=pltpu.CompilerParams(
            dimension_semantics=("arbitrary",), vmem_limit_bytes=VMEM_LIMIT),
        name="s5_ssm",
    )(u, u, bmat, cmat, a_re, a_im, d_skip, glu_w, glu_b)


def _mix_ffn_kernel(x_ref, a_ref, s_ref, wa_ref, ws_ref, g1_ref, b1_ref, w1_ref, w2_ref,
                    g2_ref, b2_ref, o_ref, *, alpha):
    a = jnp.concatenate([a_ref[h] for h in range(a_ref.shape[0])], axis=1)
    mix = _dot(a, wa_ref[...]) + _dot(s_ref[...], ws_ref[...])
    x1 = _layer_norm(alpha * x_ref[...] + mix, g1_ref[...], b1_ref[...])
    h = jnp.maximum(_dot(x1.astype(BF16), w1_ref[...]), 0.0)
    ff = _dot((h * h).astype(BF16), w2_ref[...])
    o_ref[...] = _layer_norm(alpha * x1 + ff, g2_ref[...], b2_ref[...])


def _mix_ffn(x, attn, ssm, wa, ws, g1, b1, w1, w2, g2, b2, *, alpha, tl):
    bsz, seq, d = x.shape
    aw = wa.shape[0]
    sw = ws.shape[0]
    const = lambda bb, t: (0, 0)
    resident = lambda arr: pl.BlockSpec(arr.shape, const, pipeline_mode=pl.Buffered(1))
    row = lambda v: v.reshape(1, d)
    return pl.pallas_call(
        functools.partial(_mix_ffn_kernel, alpha=alpha),
        grid=(bsz, seq // tl),
        in_specs=[
            pl.BlockSpec((None, tl, d), lambda bb, t: (bb, t, 0)),
            pl.BlockSpec((aw // V_DIM, tl, V_DIM), lambda bb, t: (bb, t, 0)),
            pl.BlockSpec((tl, sw), lambda bb, t: (t, bb)),
            resident(wa),
            resident(ws),
            pl.BlockSpec((1, d), const),
            pl.BlockSpec((1, d), const),
            resident(w1),
            resident(w2),
            pl.BlockSpec((1, d), const),
            pl.BlockSpec((1, d), const),
        ],
        out_specs=pl.BlockSpec((None, tl, d), lambda bb, t: (bb, t, 0)),
        out_shape=jax.ShapeDtypeStruct((bsz, seq, d), F32),
        compiler_params=pltpu.CompilerParams(
            dimension_semantics=("parallel", "parallel"), vmem_limit_bytes=VMEM_LIMIT),
        name="mix_ffn_ln",
    )(x, attn, ssm, wa, ws, row(g1), row(b1), w1, w2, row(g2), row(b2))


def _rope_tables(seq, q_scale):
    pos = jnp.arange(seq, dtype=F32)
    inv_freq = ROPE_THETA ** (-jnp.arange(0, HEAD_DIM, 2, dtype=F32) / HEAD_DIM)
    ang = inv_freq[:, None] * pos[None, :]
    cos = jnp.cos(ang)
    sin = jnp.sin(ang)
    return cos * q_scale, sin * q_scale, cos, sin


def _ssm_params(lam_re, lam_im, log_dt, b_re, b_im, c_re, c_im, n_half=2):
    g, p = lam_re.shape
    c = b_re.shape[-1]
    lr = lam_re.astype(F32)
    li = lam_im.astype(F32)
    dt = jnp.exp(log_dt.astype(F32))[:, None]
    mag = jnp.exp(lr * dt)
    ar = mag * jnp.cos(li * dt)
    ai = mag * jnp.sin(li * dt)
    den = lr * lr + li * li
    fr = ((ar - 1.0) * lr + ai * li) / den
    fi = (ai * lr - (ar - 1.0) * li) / den
    br = b_re.astype(F32)
    bi = b_im.astype(F32)
    bbar_re = fr[..., None] * br - fi[..., None] * bi
    bbar_im = fr[..., None] * bi + fi[..., None] * br
    gh = g // n_half
    eye = jnp.eye(gh, dtype=F32)

    def blockdiag_in(m):
        m = m.reshape(n_half, gh, p, c)
        return jnp.einsum('hgpc,gk->hgckp', m, eye).reshape(n_half, gh * c, gh * p)

    def blockdiag_out(m):
        m = m.reshape(n_half, gh, c, p)
        return jnp.einsum('hgcp,gk->hgpkc', m, eye).reshape(n_half, gh * p, gh * c)

    bmat = jnp.concatenate([blockdiag_in(bbar_re), blockdiag_in(bbar_im)], axis=2)
    cmat = jnp.concatenate([blockdiag_out(c_re.astype(F32)), blockdiag_out(-c_im.astype(F32))], axis=1)
    a_re = jnp.broadcast_to(ar.reshape(1, g * p), (SUBLANES, g * p))
    a_im = jnp.broadcast_to(ai.reshape(1, g * p), (SUBLANES, g * p))
    return bmat.astype(BF16), cmat.astype(BF16), a_re, a_im


def kernel(x, w_in, w_out, lam_qk, subln_g, ssm_lam_re, ssm_lam_im, ssm_log_dt, ssm_b_re, ssm_b_im,
           ssm_c_re, ssm_c_im, ssm_d, glu_w, glu_b, ln1_g, ln1_b, w_ff1, w_ff2, ln2_g, ln2_b):
    bsz, seq, d_model = x.shape
    depth = w_in.shape[0]
    ssm_w = glu_w.shape[1]
    attn_w = w_out.shape[1] - ssm_w
    n_heads = attn_w // V_DIM
    qk_w = n_heads * 2 * HEAD_DIM
    assert bsz == SUBLANES, "the S5 scan keeps the batch on the sublane axis"
    assert w_in.shape[2] == 2 * qk_w + attn_w + ssm_w
    alpha = (2.0 * depth) ** 0.25
    q_scale = math.log2(math.e) / math.sqrt(HEAD_DIM)

    tl = min(512, seq)
    tq = tl
    tc = min(64, seq)

    cosq, sinq, cosk, sink = _rope_tables(seq, q_scale)
    for l in range(depth):
        lam_init = 0.8 - 0.6 * math.exp(-0.3 * l)
        w = w_in[l].astype(BF16)
        wq = w[:, :qk_w]
        wk = w[:, qk_w:2 * qk_w]
        wv = w[:, 2 * qk_w:2 * qk_w + attn_w]
        wu = w[:, 2 * qk_w + attn_w:]
        qt, k, vt, u = _in_proj(x, wq.T, wk.T, wv.T, wu, cosq, sinq, cosk, sink, tl=tl)

        attn = _attention(qt, k, vt, lam_qk[l].astype(F32), subln_g[l].astype(F32),
                          bsz=bsz, n_heads=n_heads, lam_init=lam_init, tq=tq)

        bmat, cmat, a_re, a_im = _ssm_params(ssm_lam_re[l], ssm_lam_im[l], ssm_log_dt[l],
                                             ssm_b_re[l], ssm_b_im[l], ssm_c_re[l], ssm_c_im[l])
        ssm = _ssm(u, bmat, cmat, a_re, a_im,
                   ssm_d[l].astype(F32).reshape(1, ssm_w), glu_w[l].astype(BF16),
                   glu_b[l].astype(F32).reshape(1, ssm_w), tc=tc)

        wo = w_out[l].astype(BF16)
        x = _mix_ffn(x, attn, ssm, wo[:attn_w], wo[attn_w:], ln1_g[l].astype(F32),
                     ln1_b[l].astype(F32), w_ff1[l].astype(BF16), w_ff2[l].astype(BF16),
                     ln2_g[l].astype(F32), ln2_b[l].astype(F32), alpha=alpha, tl=tl)
    return x
```

```python
import functools
import math

import jax
import jax.numpy as jnp
from jax import lax
from jax.experimental import pallas as pl
from jax.experimental.pallas import tpu as pltpu

F32 = jnp.float32
BF16 = jnp.bfloat16

HEAD_DIM = 64
V_DIM = 2 * HEAD_DIM
ROPE_THETA = 10000.0
LN_EPS = 1e-5
RMS_EPS = 1e-5
LANES = 128
SUBLANES = 8
NEG_BIG = -1e30
CHUNK = 128
VMEM_LIMIT = 56 * 1024 * 1024


def _dot(a, b):
    return jnp.dot(a, b, preferred_element_type=F32)


def _layer_norm(y, g, b):
    mu = jnp.mean(y, axis=-1, keepdims=True)
    d = y - mu
    var = jnp.mean(d * d, axis=-1, keepdims=True)
    return d * lax.rsqrt(var + LN_EPS) * g + b


_NT = (((1,), (1,)), ((), ()))


def _rope_rows(h, cos, sin, out_dtype):
    half = HEAD_DIM // 2
    out = []
    for c in range(h.shape[0] // HEAD_DIM):
        x1 = h[c * HEAD_DIM:c * HEAD_DIM + half]
        x2 = h[c * HEAD_DIM + half:(c + 1) * HEAD_DIM]
        out.append((x1 * cos - x2 * sin).astype(out_dtype))
        out.append((x2 * cos + x1 * sin).astype(out_dtype))
    return out


def _in_proj_kernel(x_ref, wqt_ref, wkt_ref, wvt_ref, wu_ref, cosq_ref, sinq_ref, cosk_ref,
                    sink_ref, qt_ref, k_ref, vt_ref, u_ref):
    x = x_ref[...].astype(BF16)
    half = HEAD_DIM // 2
    hq = lax.dot_general(wqt_ref[...], x, _NT, preferred_element_type=F32)
    for i, rows in enumerate(_rope_rows(hq, cosq_ref[...], sinq_ref[...], BF16)):
        qt_ref[i * half:(i + 1) * half, :] = rows
    vt_ref[...] = lax.dot_general(wvt_ref[...], x, _NT, preferred_element_type=F32).astype(BF16)
    hk = lax.dot_general(wkt_ref[...], x, _NT, preferred_element_type=F32)
    kt = jnp.concatenate(_rope_rows(hk, cosk_ref[...], sink_ref[...], F32), axis=0)
    for j in range(k_ref.shape[0]):
        k_ref[j] = kt[j * LANES:(j + 1) * LANES].T.astype(BF16)
    u_ref[...] = _dot(x, wu_ref[...])


def _in_proj(x, wqt, wkt, wvt, wu, cosq, sinq, cosk, sink, *, tl):
    bsz, seq, d = x.shape
    qw = wqt.shape[0]
    kw = wkt.shape[0]
    vw = wvt.shape[0]
    uw = wu.shape[1]
    const = lambda b, t: (0, 0)
    table = pl.BlockSpec((HEAD_DIM // 2, tl), lambda b, t: (0, t))
    return pl.pallas_call(
        _in_proj_kernel,
        grid=(bsz, seq // tl),
        in_specs=[
            pl.BlockSpec((None, tl, d), lambda b, t: (b, t, 0)),
            pl.BlockSpec(wqt.shape, const),
            pl.BlockSpec(wkt.shape, const),
            pl.BlockSpec(wvt.shape, const),
            pl.BlockSpec(wu.shape, const),
            table, table, table, table,
        ],
        out_specs=[
            pl.BlockSpec((None, qw, tl), lambda b, t: (t, b, 0)),
            pl.BlockSpec((kw // LANES, tl, LANES), lambda b, t: (b, t, 0)),
            pl.BlockSpec((None, vw, tl), lambda b, t: (t, b, 0)),
            pl.BlockSpec((tl, uw), lambda b, t: (t, b)),
        ],
        out_shape=[
            jax.ShapeDtypeStruct((seq // tl, bsz * qw, tl), BF16),
            jax.ShapeDtypeStruct((bsz * kw // LANES, seq, LANES), BF16),
            jax.ShapeDtypeStruct((seq // tl, bsz * vw, tl), BF16),
            jax.ShapeDtypeStruct((seq, bsz * uw), F32),
        ],
        compiler_params=pltpu.CompilerParams(
            dimension_semantics=("parallel", "parallel"), vmem_limit_bytes=VMEM_LIMIT),
        name="in_proj",
    )(x, wqt, wkt, wvt, wu, cosq, sinq, cosk, sink)


def _masked_q(qt_ref, qi):
    qt = qt_ref[qi].astype(F32)
    rowid = lax.broadcasted_iota(jnp.int32, qt.shape, 0)
    return (jnp.where(rowid < HEAD_DIM, qt, 0.0).astype(BF16),
            jnp.where(rowid >= HEAD_DIM, qt, 0.0).astype(BF16))


def _scores(mp, kt, qm, k_ref, s_ref, cm_ref):
    tk, tq = s_ref.shape[1:]
    k0 = pl.multiple_of(kt * tk, tk)
    s = _dot(k_ref[pl.ds(k0, tk), :], qm[mp])
    s_ref[mp] = s
    cm_ref[mp] = jnp.max(s.reshape(tk // SUBLANES, SUBLANES, tq), axis=0)


def _attn_kernel(lq_ref, g_ref, qt_ref, k_ref, vt_ref, o_ref, m_ref, l_ref, acc_ref,
                 s_ref, cm_ref, p_ref, a_ref, *, tq, lam_init):
    n_q = qt_ref.shape[0]
    last = n_q - 1

    def scores(q_idx, kt):
        qm = _masked_q(qt_ref, q_idx)
        for mp in range(2):
            _scores(mp, kt, qm, k_ref, s_ref, cm_ref)

    def softmax_tile(mp, masked):
        if masked:
            cm = jnp.full((SUBLANES, tq), NEG_BIG, F32)
            for c in range(tq // CHUNK):
                sc = s_ref[mp, c * CHUNK:(c + 1) * CHUNK, :]
                row = lax.broadcasted_iota(jnp.int32, sc.shape, 0) + c * CHUNK
                col = lax.broadcasted_iota(jnp.int32, sc.shape, 1)
                sc = jnp.where(row <= col, sc, NEG_BIG)
                s_ref[mp, c * CHUNK:(c + 1) * CHUNK, :] = sc
                cm = jnp.maximum(cm, jnp.max(sc.reshape(CHUNK // SUBLANES, SUBLANES, tq), axis=0))
        else:
            cm = cm_ref[mp]
        m_prev = m_ref[mp]
        m_new = jnp.maximum(m_prev, jnp.max(cm, axis=0, keepdims=True))
        alpha = jnp.exp2(m_prev - m_new)
        m_ref[mp] = m_new
        lsum = jnp.zeros((SUBLANES, tq), F32)
        for c in range(tq // CHUNK):
            pc = jnp.exp2(s_ref[mp, c * CHUNK:(c + 1) * CHUNK, :] - m_new)
            lsum = lsum + jnp.sum(pc.reshape(CHUNK // SUBLANES, SUBLANES, tq), axis=0)
            p_ref[mp, c * CHUNK:(c + 1) * CHUNK, :] = pc.astype(BF16)
        l_ref[mp] = alpha * l_ref[mp] + lsum
        a_ref[mp] = alpha

    def pv(kt):
        vt = vt_ref[kt]
        for mp in range(2):
            acc_ref[mp] = a_ref[mp] * acc_ref[mp] + _dot(vt, p_ref[mp])

    def finalize(qi):
        lq = lq_ref[...]
        lam = (jnp.exp(jnp.sum(lq[0:1] * lq[1:2], axis=1, keepdims=True))
               - jnp.exp(jnp.sum(lq[2:3] * lq[3:4], axis=1, keepdims=True)) + lam_init)
        l0 = jnp.sum(l_ref[0], axis=0, keepdims=True)
        l1 = jnp.sum(l_ref[1], axis=0, keepdims=True)
        ot = acc_ref[0] / l0 - lam * (acc_ref[1] / l1)
        ms = jnp.mean(ot * ot, axis=0, keepdims=True)
        ot = ot * lax.rsqrt(ms + RMS_EPS) * g_ref[...] * (1.0 - lam_init)
        o_ref[pl.ds(pl.multiple_of(qi * tq, tq), tq), :] = ot.T.astype(o_ref.dtype)

    def last_tile_of(qi):
        return jnp.where(qi >= 1, qi - 1, qi)

    def prefetch_after(qi, kt_next):
        same = kt_next < qi
        nq = jnp.minimum(qi + 1, last)
        scores(jnp.where(same, qi, nq), jnp.where(same, kt_next, nq))

    def step(qi, kt):
        pv(jnp.where(kt == 0, qi, kt - 1))
        for mp in range(2):
            softmax_tile(mp, False)
        prefetch_after(qi, kt + 1)

    def q_tile(qi, carry):
        prev = jnp.maximum(qi - 1, 0)
        pv(last_tile_of(prev))
        finalize(prev)
        m_ref[...] = jnp.full(m_ref.shape, NEG_BIG, F32)
        l_ref[...] = jnp.zeros(l_ref.shape, F32)
        acc_ref[...] = jnp.zeros(acc_ref.shape, F32)
        for mp in range(2):
            softmax_tile(mp, True)
        prefetch_after(qi, 0)

        def body(i, c):
            step(qi, 2 * i)
            step(qi, 2 * i + 1)
            return c

        lax.fori_loop(0, qi // 2, body, 0)

        @pl.when(qi % 2 == 1)
        def _():
            step(qi, qi - 1)

        return carry

    p_ref[...] = jnp.zeros(p_ref.shape, BF16)
    a_ref[...] = jnp.ones(a_ref.shape, F32)
    acc_ref[...] = jnp.zeros(acc_ref.shape, F32)
    l_ref[...] = jnp.ones(l_ref.shape, F32)
    scores(0, 0)
    lax.fori_loop(0, n_q, q_tile, 0)
    pv(last_tile_of(last))
    finalize(last)


def _attention(qt, k, vt, lam_qk, subln_g, *, bsz, n_heads, lam_init, tq):
    seq = k.shape[1]
    return pl.pallas_call(
        functools.partial(_attn_kernel, tq=tq, lam_init=lam_init),
        grid=(bsz, n_heads),
        in_specs=[
            pl.BlockSpec(lam_qk.shape, lambda b, h: (0, 0)),
            pl.BlockSpec((V_DIM, 1), lambda b, h: (0, 0)),
            pl.BlockSpec((seq // tq, V_DIM, tq), lambda b, h: (0, b * n_heads + h, 0)),
            pl.BlockSpec((None, seq, V_DIM), lambda b, h: (b * n_heads + h, 0, 0)),
            pl.BlockSpec((seq // tq, V_DIM, tq), lambda b, h: (0, b * n_heads + h, 0)),
        ],
        out_specs=pl.BlockSpec((None, seq, V_DIM), lambda b, h: (b * n_heads + h, 0, 0)),
        out_shape=jax.ShapeDtypeStruct((bsz * n_heads, seq, V_DIM), BF16),
        scratch_shapes=[
            pltpu.VMEM((2, 1, tq), F32),
            pltpu.VMEM((2, SUBLANES, tq), F32),
            pltpu.VMEM((2, V_DIM, tq), F32),
            pltpu.VMEM((2, tq, tq), F32),
            pltpu.VMEM((2, SUBLANES, tq), F32),
            pltpu.VMEM((2, tq, tq), BF16),
            pltpu.VMEM((2, 1, tq), F32),
        ],
        compiler_params=pltpu.CompilerParams(
            dimension_semantics=("parallel", "parallel"), vmem_limit_bytes=VMEM_LIMIT),
        name="diff_attn",
    )(lam_qk, subln_g.reshape(V_DIM, 1), qt, k, vt)


def _gelu_tanh(y):
    c = math.sqrt(2.0 / math.pi)
    return 0.5 * y * (1.0 + jnp.tanh(c * (y + 0.044715 * (y * y * y))))


def _anchor(x):
    bits = pltpu.bitcast(x, jnp.uint32)
    return pltpu.bitcast(lax.shift_right_logical(bits, jnp.uint32(32)), F32)


def _ssm_kernel(u_ref, un_ref, bmat_ref, cmat_ref, ar_ref, ai_ref, d_ref, gw_ref, gb_ref,
                out_ref, urow_ref, ubf_ref, bu0_ref, bu1_ref, sb_ref, st_ref, *, tc, slab):
    n_half = bmat_ref.shape[0]
    uk = bmat_ref.shape[1]
    hw = bmat_ref.shape[2]
    cw = hw // 2
    step = pl.program_id(0)

    n_col = 2 * LANES
    bu = (bu0_ref, bu1_ref)

    def relayout(src_ref, slot):
        u = pltpu.einshape("t(bc)->(tb)c", src_ref[...], b=SUBLANES)
        urow_ref[slot] = u
        ubf_ref[slot] = u.astype(BF16)

    def project_tile(slot, j):
        hf, c = divmod(j * n_col, hw)
        bu[slot][:, hf * hw + c:hf * hw + c + n_col] = _dot(
            ubf_ref[slot, :, hf * uk:(hf + 1) * uk], bmat_ref[hf, :, c:c + n_col])

    n_tiles = n_half * hw // n_col

    @pl.when(step == 0)
    def _():
        st_ref[...] = jnp.zeros(st_ref.shape, F32)
        relayout(u_ref, 0)
        for j in range(n_tiles):
            project_tile(0, j)

    def chunk(cur, nxt):
        relayout(un_ref, nxt)
        slabs = [(hf * hw + sl * slab, hf * hw + cw + sl * slab, hf * cw + sl * slab)
                 for hf in range(n_half) for sl in range(cw // slab)]
        state = [(st_ref[:, cr:cr + slab], st_ref[:, ci:ci + slab]) for cr, ci, _ in slabs]
        n_pairs = tc // 2
        tile_at = {(j * n_pairs) // n_tiles: j for j in range(n_tiles)} if n_pairs >= n_tiles else {}
        for t2 in range(n_pairs):
            j = tile_at.get(t2)
            if j is not None:
                hf_j, c_j = divmod(j * n_col, hw)
                if j > 0:
                    z = _anchor(state[0][0][:, :LANES])
                    head = (slice(0, 2 * SUBLANES), slice(hf_j * uk, hf_j * uk + LANES))
                    ubf_ref[(nxt,) + head] = ubf_ref[(nxt,) + head] + jnp.concatenate([z, z], 0).astype(BF16)
                project_tile(nxt, j)
                if j > 0:
                    hf_p, c_p = divmod((j - 1) * n_col, hw)
                    z = _anchor(bu[nxt][0:SUBLANES, hf_p * hw + c_p:hf_p * hw + c_p + LANES])
                    zz = jnp.concatenate([z] * (slab // LANES), axis=1)
                    state = [(s_re + zz, s_im) for s_re, s_im in state]
            rows2 = slice(2 * t2 * SUBLANES, (2 * t2 + 2) * SUBLANES)
            for i, (cr, ci, ca) in enumerate(slabs):
                a_re = ar_ref[:, ca:ca + slab]
                a_im = ai_ref[:, ca:ca + slab]
                s_re, s_im = state[i]
                pair_re, pair_im = [], []
                for t in (2 * t2, 2 * t2 + 1):
                    rows = slice(t * SUBLANES, (t + 1) * SUBLANES)
                    n_re = a_re * s_re - a_im * s_im + bu[cur][rows, cr:cr + slab]
                    n_im = a_re * s_im + a_im * s_re + bu[cur][rows, ci:ci + slab]
                    s_re, s_im = n_re, n_im
                    pair_re.append(n_re)
                    pair_im.append(n_im)
                state[i] = (s_re, s_im)
                sb_ref[rows2, cr:cr + slab] = jnp.concatenate(pair_re, axis=0).astype(BF16)
                sb_ref[rows2, ci:ci + slab] = jnp.concatenate(pair_im, axis=0).astype(BF16)
        for (cr, ci, _), (s_re, s_im) in zip(slabs, state):
            st_ref[:, cr:cr + slab] = s_re
            st_ref[:, ci:ci + slab] = s_im
        if not tile_at:
            for j in range(n_tiles):
                project_tile(nxt, j)

        ys = [_dot(sb_ref[:, hf * hw:(hf + 1) * hw], cmat_ref[hf]) for hf in range(n_half)]
        y = jnp.concatenate(ys, axis=1) + d_ref[...] * urow_ref[cur]
        y = _gelu_tanh(y)
        z = _dot(y.astype(BF16), gw_ref[...]) + gb_ref[...]
        out = pltpu.einshape("(tb)c->t(bc)", y * jax.nn.sigmoid(z), b=SUBLANES)
        out_ref[...] = out.astype(out_ref.dtype)

    @pl.when(step % 2 == 0)
    def _():
        chunk(0, 1)

    @pl.when(step % 2 == 1)
    def _():
        chunk(1, 0)


def _ssm(u, bmat, cmat, a_re, a_im, d_skip, glu_w, glu_b, *, tc, slab=512):
    seq, bw = u.shape
    blk = tc * SUBLANES
    state_w = bmat.shape[0] * bmat.shape[2]
    n_chunks = seq // tc
    c2 = lambda t: (0, 0)
    c3 = lambda t: (0, 0, 0)
    return pl.pallas_call(
        functools.partial(_ssm_kernel, tc=tc, slab=slab),
        grid=(n_chunks,),
        in_specs=[
            pl.BlockSpec((tc, bw), lambda t: (t, 0)),
            pl.BlockSpec((tc, bw), lambda t: (jnp.minimum(t + 1, n_chunks - 1), 0)),
            pl.BlockSpec(bmat.shape, c3),
            pl.BlockSpec(cmat.shape, c3),
            pl.BlockSpec(a_re.shape, c2),
            pl.BlockSpec(a_im.shape, c2),
            pl.BlockSpec(d_skip.shape, c2),
            pl.BlockSpec(glu_w.shape, c2),
            pl.BlockSpec(glu_b.shape, c2),
        ],
        out_specs=pl.BlockSpec((tc, bw), lambda t: (t, 0)),
        out_shape=jax.ShapeDtypeStruct((seq, bw), BF16),
        scratch_shapes=[
            pltpu.VMEM((2, blk, bw // SUBLANES), F32),
            pltpu.VMEM((2, blk, bw // SUBLANES), BF16),
            pltpu.VMEM((blk, state_w), F32),
            pltpu.VMEM((blk, state_w), F32),
            pltpu.VMEM((blk, state_w), BF16),
            pltpu.VMEM((SUBLANES, state_w), F32),
        ],
        compiler_params=pltpu.CompilerParams(
            dimension_semantics=("arbitrary",), vmem_limit_bytes=VMEM_LIMIT),
        name="s5_ssm",
    )(u, u, bmat, cmat, a_re, a_im, d_skip, glu_w, glu_b)


def _mix_ffn_kernel(x_ref, a_ref, s_ref, wa_ref, ws_ref, g1_ref, b1_ref, w1_ref, w2_ref,
                    g2_ref, b2_ref, o_ref, *, alpha):
    a = jnp.concatenate([a_ref[h] for h in range(a_ref.shape[0])], axis=1)
    mix = _dot(a, wa_ref[...]) + _dot(s_ref[...], ws_ref[...])
    x1 = _layer_norm(alpha * x_ref[...] + mix, g1_ref[...], b1_ref[...])
    h = jnp.maximum(_dot(x1.astype(BF16), w1_ref[...]), 0.0)
    ff = _dot((h * h).astype(BF16), w2_ref[...])
    o_ref[...] = _layer_norm(alpha * x1 + ff, g2_ref[...], b2_ref[...])


def _mix_ffn(x, attn, ssm, wa, ws, g1, b1, w1, w2, g2, b2, *, alpha, tl):
    bsz, seq, d = x.shape
    aw = wa.shape[0]
    sw = ws.shape[0]
    const = lambda bb, t: (0, 0)
    resident = lambda arr: pl.BlockSpec(arr.shape, const, pipeline_mode=pl.Buffered(1))
    row = lambda v: v.reshape(1, d)
    return pl.pallas_call(
        functools.partial(_mix_ffn_kernel, alpha=alpha),
        grid=(bsz, seq // tl),
        in_specs=[
            pl.BlockSpec((None, tl, d), lambda bb, t: (bb, t, 0)),
            pl.BlockSpec((aw // V_DIM, tl, V_DIM), lambda bb, t: (bb, t, 0)),
            pl.BlockSpec((tl, sw), lambda bb, t: (t, bb)),
            resident(wa),
            resident(ws),
            pl.BlockSpec((1, d), const),
            pl.BlockSpec((1, d), const),
            resident(w1),
            resident(w2),
            pl.BlockSpec((1, d), const),
            pl.BlockSpec((1, d), const),
        ],
        out_specs=pl.BlockSpec((None, tl, d), lambda bb, t: (bb, t, 0)),
        out_shape=jax.ShapeDtypeStruct((bsz, seq, d), F32),
        compiler_params=pltpu.CompilerParams(
            dimension_semantics=("parallel", "parallel"), vmem_limit_bytes=VMEM_LIMIT),
        name="mix_ffn_ln",
    )(x, attn, ssm, wa, ws, row(g1), row(b1), w1, w2, row(g2), row(b2))


def _rope_tables(seq, q_scale):
    pos = jnp.arange(seq, dtype=F32)
    inv_freq = ROPE_THETA ** (-jnp.arange(0, HEAD_DIM, 2, dtype=F32) / HEAD_DIM)
    ang = inv_freq[:, None] * pos[None, :]
    cos = jnp.cos(ang)
    sin = jnp.sin(ang)
    return cos * q_scale, sin * q_scale, cos, sin


def _ssm_params(lam_re, lam_im, log_dt, b_re, b_im, c_re, c_im, n_half=2):
    g, p = lam_re.shape
    c = b_re.shape[-1]
    lr = lam_re.astype(F32)
    li = lam_im.astype(F32)
    dt = jnp.exp(log_dt.astype(F32))[:, None]
    mag = jnp.exp(lr * dt)
    ar = mag * jnp.cos(li * dt)
    ai = mag * jnp.sin(li * dt)
    den = lr * lr + li * li
    fr = ((ar - 1.0) * lr + ai * li) / den
    fi = (ai * lr - (ar - 1.0) * li) / den
    br = b_re.astype(F32)
    bi = b_im.astype(F32)
    bbar_re = fr[..., None] * br - fi[..., None] * bi
    bbar_im = fr[..., None] * bi + fi[..., None] * br
    gh = g // n_half
    eye = jnp.eye(gh, dtype=F32)

    def blockdiag_in(m):
        m = m.reshape(n_half, gh, p, c)
        return jnp.einsum('hgpc,gk->hgckp', m, eye).reshape(n_half, gh * c, gh * p)

    def blockdiag_out(m):
        m = m.reshape(n_half, gh, c, p)
        return jnp.einsum('hgcp,gk->hgpkc', m, eye).reshape(n_half, gh * p, gh * c)

    bmat = jnp.concatenate([blockdiag_in(bbar_re), blockdiag_in(bbar_im)], axis=2)
    cmat = jnp.concatenate([blockdiag_out(c_re.astype(F32)), blockdiag_out(-c_im.astype(F32))], axis=1)
    a_re = jnp.broadcast_to(ar.reshape(1, g * p), (SUBLANES, g * p))
    a_im = jnp.broadcast_to(ai.reshape(1, g * p), (SUBLANES, g * p))
    return bmat.astype(BF16), cmat.astype(BF16), a_re, a_im


def kernel(x, w_in, w_out, lam_qk, subln_g, ssm_lam_re, ssm_lam_im, ssm_log_dt, ssm_b_re, ssm_b_im,
           ssm_c_re, ssm_c_im, ssm_d, glu_w, glu_b, ln1_g, ln1_b, w_ff1, w_ff2, ln2_g, ln2_b):
    bsz, seq, d_model = x.shape
    depth = w_in.shape[0]
    ssm_w = glu_w.shape[1]
    attn_w = w_out.shape[1] - ssm_w
    n_heads = attn_w // V_DIM
    qk_w = n_heads * 2 * HEAD_DIM
    assert bsz == SUBLANES, "the S5 scan keeps the batch on the sublane axis"
    assert w_in.shape[2] == 2 * qk_w + attn_w + ssm_w
    alpha = (2.0 * depth) ** 0.25
    q_scale = math.log2(math.e) / math.sqrt(HEAD_DIM)

    tl = min(512, seq)
    tq = tl
    tc = min(64, seq)

    cosq, sinq, cosk, sink = _rope_tables(seq, q_scale)
    for l in range(depth):
        lam_init = 0.8 - 0.6 * math.exp(-0.3 * l)
        w = w_in[l].astype(BF16)
        wq = w[:, :qk_w]
        wk = w[:, qk_w:2 * qk_w]
        wv = w[:, 2 * qk_w:2 * qk_w + attn_w]
        wu = w[:, 2 * qk_w + attn_w:]
        qt, k, vt, u = _in_proj(x, wq.T, wk.T, wv.T, wu, cosq, sinq, cosk, sink, tl=tl)

        attn = _attention(qt, k, vt, lam_qk[l].astype(F32), subln_g[l].astype(F32),
                          bsz=bsz, n_heads=n_heads, lam_init=lam_init, tq=tq)

        bmat, cmat, a_re, a_im = _ssm_params(ssm_lam_re[l], ssm_lam_im[l], ssm_log_dt[l],
                                             ssm_b_re[l], ssm_b_im[l], ssm_c_re[l], ssm_c_im[l])
        ssm = _ssm(u, bmat, cmat, a_re, a_im,
                   ssm_d[l].astype(F32).reshape(1, ssm_w), glu_w[l].astype(BF16),
                   glu_b[l].astype(F32).reshape(1, ssm_w), tc=tc)

        wo = w_out[l].astype(BF16)
        x = _mix_ffn(x, attn, ssm, wo[:attn_w], wo[attn_w:], ln1_g[l].astype(F32),
                     ln1_b[l].astype(F32), w_ff1[l].astype(BF16), w_ff2[l].astype(BF16),
                     ln2_g[l].astype(F32), ln2_b[l].astype(F32), alpha=alpha, tl=tl)
    return x
```

```python
import functools
import math

import jax
import jax.numpy as jnp
from jax import lax
from jax.experimental import pallas as pl
from jax.experimental.pallas import tpu as pltpu

F32 = jnp.float32
BF16 = jnp.bfloat16

HEAD_DIM = 64
V_DIM = 2 * HEAD_DIM
ROPE_THETA = 10000.0
LN_EPS = 1e-5
RMS_EPS = 1e-5
LANES = 128
SUBLANES = 8
NEG_BIG = -1e30
CHUNK = 128
VMEM_LIMIT = 56 * 1024 * 1024


def _dot(a, b):
    return jnp.dot(a, b, preferred_element_type=F32)


def _layer_norm(y, g, b):
    mu = jnp.mean(y, axis=-1, keepdims=True)
    d = y - mu
    var = jnp.mean(d * d, axis=-1, keepdims=True)
    return d * lax.rsqrt(var + LN_EPS) * g + b


_NT = (((1,), (1,)), ((), ()))


def _rope_rows(h, cos, sin, out_dtype):
    half = HEAD_DIM // 2
    out = []
    for c in range(h.shape[0] // HEAD_DIM):
        x1 = h[c * HEAD_DIM:c * HEAD_DIM + half]
        x2 = h[c * HEAD_DIM + half:(c + 1) * HEAD_DIM]
        out.append((x1 * cos - x2 * sin).astype(out_dtype))
        out.append((x2 * cos + x1 * sin).astype(out_dtype))
    return out


def _in_proj_kernel(x_ref, wqt_ref, wkt_ref, wvt_ref, wu_ref, cosq_ref, sinq_ref, cosk_ref,
                    sink_ref, qt_ref, k_ref, vt_ref, u_ref):
    x = x_ref[...].astype(BF16)
    half = HEAD_DIM // 2
    hq = lax.dot_general(wqt_ref[...], x, _NT, preferred_element_type=F32)
    for i, rows in enumerate(_rope_rows(hq, cosq_ref[...], sinq_ref[...], BF16)):
        qt_ref[i * half:(i + 1) * half, :] = rows
    vt_ref[...] = lax.dot_general(wvt_ref[...], x, _NT, preferred_element_type=F32).astype(BF16)
    hk = lax.dot_general(wkt_ref[...], x, _NT, preferred_element_type=F32)
    kt = jnp.concatenate(_rope_rows(hk, cosk_ref[...], sink_ref[...], F32), axis=0)
    for j in range(k_ref.shape[0]):
        k_ref[j] = kt[j * LANES:(j + 1) * LANES].T.astype(BF16)
    u_ref[...] = _dot(x, wu_ref[...])


def _in_proj(x, wqt, wkt, wvt, wu, cosq, sinq, cosk, sink, *, tl):
    bsz, seq, d = x.shape
    qw = wqt.shape[0]
    kw = wkt.shape[0]
    vw = wvt.shape[0]
    uw = wu.shape[1]
    const = lambda b, t: (0, 0)
    table = pl.BlockSpec((HEAD_DIM // 2, tl), lambda b, t: (0, t))
    return pl.pallas_call(
        _in_proj_kernel,
        grid=(bsz, seq // tl),
        in_specs=[
            pl.BlockSpec((None, tl, d), lambda b, t: (b, t, 0)),
            pl.BlockSpec(wqt.shape, const),
            pl.BlockSpec(wkt.shape, const),
            pl.BlockSpec(wvt.shape, const),
            pl.BlockSpec(wu.shape, const),
            table, table, table, table,
        ],
        out_specs=[
            pl.BlockSpec((None, qw, tl), lambda b, t: (t, b, 0)),
            pl.BlockSpec((kw // LANES, tl, LANES), lambda b, t: (b, t, 0)),
            pl.BlockSpec((None, vw, tl), lambda b, t: (t, b, 0)),
            pl.BlockSpec((tl, uw), lambda b, t: (t, b)),
        ],
        out_shape=[
            jax.ShapeDtypeStruct((seq // tl, bsz * qw, tl), BF16),
            jax.ShapeDtypeStruct((bsz * kw // LANES, seq, LANES), BF16),
            jax.ShapeDtypeStruct((seq // tl, bsz * vw, tl), BF16),
            jax.ShapeDtypeStruct((seq, bsz * uw), F32),
        ],
        compiler_params=pltpu.CompilerParams(
            dimension_semantics=("parallel", "parallel"), vmem_limit_bytes=VMEM_LIMIT),
        name="in_proj",
    )(x, wqt, wkt, wvt, wu, cosq, sinq, cosk, sink)


def _masked_q(qt_ref, qi):
    qt = qt_ref[qi].astype(F32)
    rowid = lax.broadcasted_iota(jnp.int32, qt.shape, 0)
    return (jnp.where(rowid < HEAD_DIM, qt, 0.0).astype(BF16),
            jnp.where(rowid >= HEAD_DIM, qt, 0.0).astype(BF16))


def _scores(mp, kt, qm, k_ref, s_ref, cm_ref):
    tk, tq = s_ref.shape[1:]
    k0 = pl.multiple_of(kt * tk, tk)
    s = _dot(k_ref[pl.ds(k0, tk), :], qm[mp])
    s_ref[mp] = s
    cm_ref[mp] = jnp.max(s.reshape(tk // SUBLANES, SUBLANES, tq), axis=0)


def _attn_kernel(lq_ref, g_ref, qt_ref, k_ref, vt_ref, o_ref, m_ref, l_ref, acc_ref,
                 s_ref, cm_ref, p_ref, a_ref, *, tq, lam_init):
    n_q = qt_ref.shape[0]
    last = n_q - 1

    def scores(q_idx, kt):
        qm = _masked_q(qt_ref, q_idx)
        for mp in range(2):
            _scores(mp, kt, qm, k_ref, s_ref, cm_ref)

    def softmax_tile(mp, masked):
        if masked:
            cm = jnp.full((SUBLANES, tq), NEG_BIG, F32)
            for c in range(tq // CHUNK):
                sc = s_ref[mp, c * CHUNK:(c + 1) * CHUNK, :]
                row = lax.broadcasted_iota(jnp.int32, sc.shape, 0) + c * CHUNK
                col = lax.broadcasted_iota(jnp.int32, sc.shape, 1)
                sc = jnp.where(row <= col, sc, NEG_BIG)
                s_ref[mp, c * CHUNK:(c + 1) * CHUNK, :] = sc
                cm = jnp.maximum(cm, jnp.max(sc.reshape(CHUNK // SUBLANES, SUBLANES, tq), axis=0))
        else:
            cm = cm_ref[mp]
        m_prev = m_ref[mp]
        m_new = jnp.maximum(m_prev, jnp.max(cm, axis=0, keepdims=True))
        alpha = jnp.exp2(m_prev - m_new)
        m_ref[mp] = m_new
        lsum = jnp.zeros((SUBLANES, tq), F32)
        for c in range(tq // CHUNK):
            pc = jnp.exp2(s_ref[mp, c * CHUNK:(c + 1) * CHUNK, :] - m_new)
            lsum = lsum + jnp.sum(pc.reshape(CHUNK // SUBLANES, SUBLANES, tq), axis=0)
            p_ref[mp, c * CHUNK:(c + 1) * CHUNK, :] = pc.astype(BF16)
        l_ref[mp] = alpha * l_ref[mp] + lsum
        a_ref[mp] = alpha

    def pv(kt):
        vt = vt_ref[kt]
        for mp in range(2):
            acc_ref[mp] = a_ref[mp] * acc_ref[mp] + _dot(vt, p_ref[mp])

    def finalize(qi):
        lq = lq_ref[...]
        lam = (jnp.exp(jnp.sum(lq[0:1] * lq[1:2], axis=1, keepdims=True))
               - jnp.exp(jnp.sum(lq[2:3] * lq[3:4], axis=1, keepdims=True)) + lam_init)
        l0 = jnp.sum(l_ref[0], axis=0, keepdims=True)
        l1 = jnp.sum(l_ref[1], axis=0, keepdims=True)
        ot = acc_ref[0] / l0 - lam * (acc_ref[1] / l1)
        ms = jnp.mean(ot * ot, axis=0, keepdims=True)
        ot = ot * lax.rsqrt(ms + RMS_EPS) * g_ref[...] * (1.0 - lam_init)
        o_ref[pl.ds(pl.multiple_of(qi * tq, tq), tq), :] = ot.T.astype(o_ref.dtype)

    def last_tile_of(qi):
        return jnp.where(qi >= 1, qi - 1, qi)

    def prefetch_after(qi, kt_next):
        same = kt_next < qi
        nq = jnp.minimum(qi + 1, last)
        scores(jnp.where(same, qi, nq), jnp.where(same, kt_next, nq))

    def step(qi, kt):
        pv(jnp.where(kt == 0, qi, kt - 1))
        for mp in range(2):
            softmax_tile(mp, False)
        prefetch_after(qi, kt + 1)

    def q_tile(qi, carry):
        def first_block(n_first):
            prev = jnp.maximum(qi - 1, 0)
            pv(last_tile_of(prev))
            finalize(prev)
            m_ref[...] = jnp.full(m_ref.shape, NEG_BIG, F32)
            l_ref[...] = jnp.zeros(l_ref.shape, F32)
            acc_ref[...] = jnp.zeros(acc_ref.shape, F32)
            for mp in range(2):
                softmax_tile(mp, True)
            prefetch_after(qi, 0)
            for kt in range(n_first):
                step(qi, kt)

        @pl.when(qi == 0)
        def _():
            first_block(0)

        @pl.when(qi >= 1)
        def _():
            first_block(1)

        def body(i, c):
            step(qi, 2 * i + 1)
            step(qi, 2 * i + 2)
            return c

        lax.fori_loop(0, (qi - 1) // 2, body, 0)

        @pl.when((qi >= 2) & (qi % 2 == 0))
        def _():
            step(qi, qi - 1)

        return carry

    p_ref[...] = jnp.zeros(p_ref.shape, BF16)
    a_ref[...] = jnp.ones(a_ref.shape, F32)
    acc_ref[...] = jnp.zeros(acc_ref.shape, F32)
    l_ref[...] = jnp.ones(l_ref.shape, F32)
    scores(0, 0)
    lax.fori_loop(0, n_q, q_tile, 0)
    pv(last_tile_of(last))
    finalize(last)


def _attention(qt, k, vt, lam_qk, subln_g, *, bsz, n_heads, lam_init, tq):
    seq = k.shape[1]
    return pl.pallas_call(
        functools.partial(_attn_kernel, tq=tq, lam_init=lam_init),
        grid=(bsz, n_heads),
        in_specs=[
            pl.BlockSpec(lam_qk.shape, lambda b, h: (0, 0)),
            pl.BlockSpec((V_DIM, 1), lambda b, h: (0, 0)),
            pl.BlockSpec((seq // tq, V_DIM, tq), lambda b, h: (0, b * n_heads + h, 0)),
            pl.BlockSpec((None, seq, V_DIM), lambda b, h: (b * n_heads + h, 0, 0)),
            pl.BlockSpec((seq // tq, V_DIM, tq), lambda b, h: (0, b * n_heads + h, 0)),
        ],
        out_specs=pl.BlockSpec((None, seq, V_DIM), lambda b, h: (b * n_heads + h, 0, 0)),
        out_shape=jax.ShapeDtypeStruct((bsz * n_heads, seq, V_DIM), BF16),
        scratch_shapes=[
            pltpu.VMEM((2, 1, tq), F32),
            pltpu.VMEM((2, SUBLANES, tq), F32),
            pltpu.VMEM((2, V_DIM, tq), F32),
            pltpu.VMEM((2, tq, tq), F32),
            pltpu.VMEM((2, SUBLANES, tq), F32),
            pltpu.VMEM((2, tq, tq), BF16),
            pltpu.VMEM((2, 1, tq), F32),
        ],
        compiler_params=pltpu.CompilerParams(
            dimension_semantics=("parallel", "parallel"), vmem_limit_bytes=VMEM_LIMIT),
        name="diff_attn",
    )(lam_qk, subln_g.reshape(V_DIM, 1), qt, k, vt)


def _gelu_tanh(y):
    c = math.sqrt(2.0 / math.pi)
    return 0.5 * y * (1.0 + jnp.tanh(c * (y + 0.044715 * (y * y * y))))


def _anchor(x):
    bits = pltpu.bitcast(x, jnp.uint32)
    return pltpu.bitcast(lax.shift_right_logical(bits, jnp.uint32(32)), F32)


def _ssm_kernel(u_ref, un_ref, bmat_ref, cmat_ref, ar_ref, ai_ref, d_ref, gw_ref, gb_ref,
                out_ref, urow_ref, ubf_ref, bu0_ref, bu1_ref, sb_ref, st_ref, *, tc, slab):
    n_half = bmat_ref.shape[0]
    uk = bmat_ref.shape[1]
    hw = bmat_ref.shape[2]
    cw = hw // 2
    step = pl.program_id(0)

    n_col = 2 * LANES
    bu = (bu0_ref, bu1_ref)

    def relayout(src_ref, slot):
        u = pltpu.einshape("t(bc)->(tb)c", src_ref[...], b=SUBLANES)
        urow_ref[slot] = u
        ubf_ref[slot] = u.astype(BF16)

    def project_tile(slot, j):
        hf, c = divmod(j * n_col, hw)
        bu[slot][:, hf * hw + c:hf * hw + c + n_col] = _dot(
            ubf_ref[slot, :, hf * uk:(hf + 1) * uk], bmat_ref[hf, :, c:c + n_col])

    n_tiles = n_half * hw // n_col

    @pl.when(step == 0)
    def _():
        st_ref[...] = jnp.zeros(st_ref.shape, F32)
        relayout(u_ref, 0)
        for j in range(n_tiles):
            project_tile(0, j)

    def chunk(cur, nxt):
        relayout(un_ref, nxt)
        slabs = [(hf * hw + sl * slab, hf * hw + cw + sl * slab, hf * cw + sl * slab)
                 for hf in range(n_half) for sl in range(cw // slab)]
        state = [(st_ref[:, cr:cr + slab], st_ref[:, ci:ci + slab]) for cr, ci, _ in slabs]
        n_pairs = tc // 2
        tile_at = {(j * n_pairs) // n_tiles: j for j in range(n_tiles)} if n_pairs >= n_tiles else {}
        for t2 in range(n_pairs):
            j = tile_at.get(t2)
            if j is not None:
                hf_j, c_j = divmod(j * n_col, hw)
                if j > 0:
                    z = _anchor(state[0][0][:, :LANES])
                    head = (slice(0, 2 * SUBLANES), slice(hf_j * uk, hf_j * uk + LANES))
                    ubf_ref[(nxt,) + head] = ubf_ref[(nxt,) + head] + jnp.concatenate([z, z], 0).astype(BF16)
                project_tile(nxt, j)
                if j > 0:
                    hf_p, c_p = divmod((j - 1) * n_col, hw)
                    z = _anchor(bu[nxt][0:SUBLANES, hf_p * hw + c_p:hf_p * hw + c_p + LANES])
                    zz = jnp.concatenate([z] * (slab // LANES), axis=1)
                    state = [(s_re + zz, s_im) for s_re, s_im in state]
            rows2 = slice(2 * t2 * SUBLANES, (2 * t2 + 2) * SUBLANES)
            for i, (cr, ci, ca) in enumerate(slabs):
                a_re = ar_ref[:, ca:ca + slab]
                a_im = ai_ref[:, ca:ca + slab]
                s_re, s_im = state[i]
                pair_re, pair_im = [], []
                for t in (2 * t2, 2 * t2 + 1):
                    rows = slice(t * SUBLANES, (t + 1) * SUBLANES)
                    n_re = a_re * s_re - a_im * s_im + bu[cur][rows, cr:cr + slab]
                    n_im = a_re * s_im + a_im * s_re + bu[cur][rows, ci:ci + slab]
                    s_re, s_im = n_re, n_im
                    pair_re.append(n_re)
                    pair_im.append(n_im)
                state[i] = (s_re, s_im)
                sb_ref[rows2, cr:cr + slab] = jnp.concatenate(pair_re, axis=0).astype(BF16)
                sb_ref[rows2, ci:ci + slab] = jnp.concatenate(pair_im, axis=0).astype(BF16)
        for (cr, ci, _), (s_re, s_im) in zip(slabs, state):
            st_ref[:, cr:cr + slab] = s_re
            st_ref[:, ci:ci + slab] = s_im
        if not tile_at:
            for j in range(n_tiles):
                project_tile(nxt, j)

        ys = [_dot(sb_ref[:, hf * hw:(hf + 1) * hw], cmat_ref[hf]) for hf in range(n_half)]
        y = jnp.concatenate(ys, axis=1) + d_ref[...] * urow_ref[cur]
        y = _gelu_tanh(y)
        z = _dot(y.astype(BF16), gw_ref[...]) + gb_ref[...]
        out = pltpu.einshape("(tb)c->t(bc)", y * jax.nn.sigmoid(z), b=SUBLANES)
        out_ref[...] = out.astype(out_ref.dtype)

    @pl.when(step % 2 == 0)
    def _():
        chunk(0, 1)

    @pl.when(step % 2 == 1)
    def _():
        chunk(1, 0)


def _ssm(u, bmat, cmat, a_re, a_im, d_skip, glu_w, glu_b, *, tc, slab=512):
    seq, bw = u.shape
    blk = tc * SUBLANES
    state_w = bmat.shape[0] * bmat.shape[2]
    n_chunks = seq // tc
    c2 = lambda t: (0, 0)
    c3 = lambda t: (0, 0, 0)
    return pl.pallas_call(
        functools.partial(_ssm_kernel, tc=tc, slab=slab),
        grid=(n_chunks,),
        in_specs=[
            pl.BlockSpec((tc, bw), lambda t: (t, 0)),
            pl.BlockSpec((tc, bw), lambda t: (jnp.minimum(t + 1, n_chunks - 1), 0)),
            pl.BlockSpec(bmat.shape, c3),
            pl.BlockSpec(cmat.shape, c3),
            pl.BlockSpec(a_re.shape, c2),
            pl.BlockSpec(a_im.shape, c2),
            pl.BlockSpec(d_skip.shape, c2),
            pl.BlockSpec(glu_w.shape, c2),
            pl.BlockSpec(glu_b.shape, c2),
        ],
        out_specs=pl.BlockSpec((tc, bw), lambda t: (t, 0)),
        out_shape=jax.ShapeDtypeStruct((seq, bw), BF16),
        scratch_shapes=[
            pltpu.VMEM((2, blk, bw // SUBLANES), F32),
            pltpu.VMEM((2, blk, bw // SUBLANES), BF16),
            pltpu.VMEM((blk, state_w), F32),
            pltpu.VMEM((blk, state_w), F32),
            pltpu.VMEM((blk, state_w), BF16),
            pltpu.VMEM((SUBLANES, state_w), F32),
        ],
        compiler_params=pltpu.CompilerParams(
            dimension_semantics=("arbitrary",), vmem_limit_bytes=VMEM_LIMIT),
        name="s5_ssm",
    )(u, u, bmat, cmat, a_re, a_im, d_skip, glu_w, glu_b)


def _mix_ffn_kernel(x_ref, a_ref, s_ref, wa_ref, ws_ref, g1_ref, b1_ref, w1_ref, w2_ref,
                    g2_ref, b2_ref, o_ref, *, alpha):
    a = jnp.concatenate([a_ref[h] for h in range(a_ref.shape[0])], axis=1)
    mix = _dot(a, wa_ref[...]) + _dot(s_ref[...], ws_ref[...])
    x1 = _layer_norm(alpha * x_ref[...] + mix, g1_ref[...], b1_ref[...])
    h = jnp.maximum(_dot(x1.astype(BF16), w1_ref[...]), 0.0)
    ff = _dot((h * h).astype(BF16), w2_ref[...])
    o_ref[...] = _layer_norm(alpha * x1 + ff, g2_ref[...], b2_ref[...])


def _mix_ffn(x, attn, ssm, wa, ws, g1, b1, w1, w2, g2, b2, *, alpha, tl):
    bsz, seq, d = x.shape
    aw = wa.shape[0]
    sw = ws.shape[0]
    const = lambda bb, t: (0, 0)
    resident = lambda arr: pl.BlockSpec(arr.shape, const, pipeline_mode=pl.Buffered(1))
    row = lambda v: v.reshape(1, d)
    return pl.pallas_call(
        functools.partial(_mix_ffn_kernel, alpha=alpha),
        grid=(bsz, seq // tl),
        in_specs=[
            pl.BlockSpec((None, tl, d), lambda bb, t: (bb, t, 0)),
            pl.BlockSpec((aw // V_DIM, tl, V_DIM), lambda bb, t: (bb, t, 0)),
            pl.BlockSpec((tl, sw), lambda bb, t: (t, bb)),
            resident(wa),
            resident(ws),
            pl.BlockSpec((1, d), const),
            pl.BlockSpec((1, d), const),
            resident(w1),
            resident(w2),
            pl.BlockSpec((1, d), const),
            pl.BlockSpec((1, d), const),
        ],
        out_specs=pl.BlockSpec((None, tl, d), lambda bb, t: (bb, t, 0)),
        out_shape=jax.ShapeDtypeStruct((bsz, seq, d), F32),
        compiler_params=pltpu.CompilerParams(
            dimension_semantics=("parallel", "parallel"), vmem_limit_bytes=VMEM_LIMIT),
        name="mix_ffn_ln",
    )(x, attn, ssm, wa, ws, row(g1), row(b1), w1, w2, row(g2), row(b2))


def _rope_tables(seq, q_scale):
    pos = jnp.arange(seq, dtype=F32)
    inv_freq = ROPE_THETA ** (-jnp.arange(0, HEAD_DIM, 2, dtype=F32) / HEAD_DIM)
    ang = inv_freq[:, None] * pos[None, :]
    cos = jnp.cos(ang)
    sin = jnp.sin(ang)
    return cos * q_scale, sin * q_scale, cos, sin


def _ssm_params(lam_re, lam_im, log_dt, b_re, b_im, c_re, c_im, n_half=2):
    g, p = lam_re.shape
    c = b_re.shape[-1]
    lr = lam_re.astype(F32)
    li = lam_im.astype(F32)
    dt = jnp.exp(log_dt.astype(F32))[:, None]
    mag = jnp.exp(lr * dt)
    ar = mag * jnp.cos(li * dt)
    ai = mag * jnp.sin(li * dt)
    den = lr * lr + li * li
    fr = ((ar - 1.0) * lr + ai * li) / den
    fi = (ai * lr - (ar - 1.0) * li) / den
    br = b_re.astype(F32)
    bi = b_im.astype(F32)
    bbar_re = fr[..., None] * br - fi[..., None] * bi
    bbar_im = fr[..., None] * bi + fi[..., None] * br
    gh = g // n_half
    eye = jnp.eye(gh, dtype=F32)

    def blockdiag_in(m):
        m = m.reshape(n_half, gh, p, c)
        return jnp.einsum('hgpc,gk->hgckp', m, eye).reshape(n_half, gh * c, gh * p)

    def blockdiag_out(m):
        m = m.reshape(n_half, gh, c, p)
        return jnp.einsum('hgcp,gk->hgpkc', m, eye).reshape(n_half, gh * p, gh * c)

    bmat = jnp.concatenate([blockdiag_in(bbar_re), blockdiag_in(bbar_im)], axis=2)
    cmat = jnp.concatenate([blockdiag_out(c_re.astype(F32)), blockdiag_out(-c_im.astype(F32))], axis=1)
    a_re = jnp.broadcast_to(ar.reshape(1, g * p), (SUBLANES, g * p))
    a_im = jnp.broadcast_to(ai.reshape(1, g * p), (SUBLANES, g * p))
    return bmat.astype(BF16), cmat.astype(BF16), a_re, a_im


def kernel(x, w_in, w_out, lam_qk, subln_g, ssm_lam_re, ssm_lam_im, ssm_log_dt, ssm_b_re, ssm_b_im,
           ssm_c_re, ssm_c_im, ssm_d, glu_w, glu_b, ln1_g, ln1_b, w_ff1, w_ff2, ln2_g, ln2_b):
    bsz, seq, d_model = x.shape
    depth = w_in.shape[0]
    ssm_w = glu_w.shape[1]
    attn_w = w_out.shape[1] - ssm_w
    n_heads = attn_w // V_DIM
    qk_w = n_heads * 2 * HEAD_DIM
    assert bsz == SUBLANES, "the S5 scan keeps the batch on the sublane axis"
    assert w_in.shape[2] == 2 * qk_w + attn_w + ssm_w
    alpha = (2.0 * depth) ** 0.25
    q_scale = math.log2(math.e) / math.sqrt(HEAD_DIM)

    tl = min(512, seq)
    tq = tl
    tc = min(64, seq)

    cosq, sinq, cosk, sink = _rope_tables(seq, q_scale)
    for l in range(depth):
        lam_init = 0.8 - 0.6 * math.exp(-0.3 * l)
        w = w_in[l].astype(BF16)
        wq = w[:, :qk_w]
        wk = w[:, qk_w:2 * qk_w]
        wv = w[:, 2 * qk_w:2 * qk_w + attn_w]
        wu = w[:, 2 * qk_w + attn_w:]
        qt, k, vt, u = _in_proj(x, wq.T, wk.T, wv.T, wu, cosq, sinq, cosk, sink, tl=tl)

        attn = _attention(qt, k, vt, lam_qk[l].astype(F32), subln_g[l].astype(F32),
                          bsz=bsz, n_heads=n_heads, lam_init=lam_init, tq=tq)

        bmat, cmat, a_re, a_im = _ssm_params(ssm_lam_re[l], ssm_lam_im[l], ssm_log_dt[l],
                                             ssm_b_re[l], ssm_b_im[l], ssm_c_re[l], ssm_c_im[l])
        ssm = _ssm(u, bmat, cmat, a_re, a_im,
                   ssm_d[l].astype(F32).reshape(1, ssm_w), glu_w[l].astype(BF16),
                   glu_b[l].astype(F32).reshape(1, ssm_w), tc=tc)

        wo = w_out[l].astype(BF16)
        x = _mix_ffn(x, attn, ssm, wo[:attn_w], wo[attn_w:], ln1_g[l].astype(F32),
                     ln1_b[l].astype(F32), w_ff1[l].astype(BF16), w_ff2[l].astype(BF16),
                     ln2_g[l].astype(F32), ln2_b[l].astype(F32), alpha=alpha, tl=tl)
    return x
```
